```python
import math
import jax, jax.numpy as jnp
from jax import lax
import numpy as np

D_MODEL = 1024
BATCH = 2
SEQ = 8192
DEPTH = 2

CHUNK = 64
DN_HEADS = 8
DN_DK = 128
DN_DV = 128
DN_QK_W = DN_HEADS * DN_DK
DN_V_W = DN_HEADS * DN_DV
CONV_K = 4
SGU_BLOCK = 128
SGU_GROUPS = 8
SGU_GROUP_DIM = 128
SGU_W = SGU_GROUPS * SGU_GROUP_DIM
FFN_HIDDEN = int(math.ceil(8 * D_MODEL / 3 / 256)) * 256
DEEPNORM_ALPHA = (2 * DEPTH) ** 0.25
DEEPNORM_BETA = (8 * DEPTH) ** -0.25
IN_SIZES = (DN_QK_W, DN_QK_W, DN_V_W,
            DN_V_W,
            DN_HEADS, DN_HEADS,
            SGU_W, SGU_W,
            D_MODEL, D_MODEL)
N_IN = sum(IN_SIZES)
LN_EPS = 1e-5
RMS_EPS = 1e-6

kernel_name = "hybrid_deltanet_sgu_deepnorm"


def layer_norm(x, g, b):
    xf = x.astype(jnp.float32)
    mu = jnp.mean(xf, -1, keepdims=True)
    var = jnp.mean(jnp.square(xf - mu), -1, keepdims=True)
    return ((xf - mu) * lax.rsqrt(var + LN_EPS) * g + b).astype(x.dtype)


def l2norm(x):
    xf = x.astype(jnp.float32)
    return xf * lax.rsqrt(jnp.sum(xf * xf, -1, keepdims=True) + RMS_EPS)


def causal_depthwise_conv(x, w):
    K, C = w.shape
    return lax.conv_general_dilated(
        x, w[:, None, :], window_strides=(1,), padding=[(K - 1, 0)],
        dimension_numbers=("NWC", "WIO", "NWC"), feature_group_count=C)


def gated_delta_rule(q, k, v, g, beta):
    f32 = jnp.float32
    B, S, H, dk = q.shape
    dv = v.shape[-1]
    N = S // CHUNK

    def chunks(t):
        t = t.astype(f32).reshape((B, N, CHUNK, H) + t.shape[3:])
        return jnp.moveaxis(t, 3, 1)

    q, k, v, g, beta = (chunks(t) for t in (q, k, v, g, beta))
    q = q * (dk ** -0.5)
    g = jnp.cumsum(g, axis=-1)
    idx = jnp.arange(CHUNK)
    tril = idx[:, None] >= idx[None, :]
    strict = idx[:, None] > idx[None, :]
    decay = jnp.exp(jnp.where(tril, g[..., :, None] - g[..., None, :], -jnp.inf))
    k_beta = k * beta[..., None]
    lower = jnp.einsum("bhnid,bhnjd->bhnij", k_beta, k) * decay
    lower = jnp.where(strict, lower, 0.0) + jnp.eye(CHUNK, dtype=f32)
    rhs = jnp.concatenate([v * beta[..., None], k_beta * jnp.exp(g)[..., None]], -1)
    sol = lax.linalg.triangular_solve(lower, rhs, left_side=True, lower=True,
                                      unit_diagonal=True)
    u_c, w_c = sol[..., :dv], sol[..., dv:]
    attn = jnp.einsum("bhnid,bhnjd->bhnij", q, k) * decay
    q_g = q * jnp.exp(g)[..., None]
    g_last = g[..., -1]
    k_dec = k * jnp.exp(g_last[..., None] - g)[..., None]

    def step(state, xs):
        qg_i, kd_i, u_i, w_i, a_i, gl_i = xs
        v_new = u_i - jnp.einsum("bhck,bhkv->bhcv", w_i, state)
        o_i = (jnp.einsum("bhck,bhkv->bhcv", qg_i, state)
               + jnp.einsum("bhij,bhjv->bhiv", a_i, v_new))
        state = (state * jnp.exp(gl_i)[..., None, None]
                 + jnp.einsum("bhck,bhcv->bhkv", kd_i, v_new))
        return state, o_i

    xs = tuple(jnp.moveaxis(t, 2, 0) for t in (q_g, k_dec, u_c, w_c, attn, g_last))
    state0 = jnp.zeros((B, H, dk, dv), f32)
    _, o = lax.scan(step, state0, xs)
    return jnp.transpose(o, (1, 0, 3, 2, 4)).reshape(B, S, H, dv)


def hybrid_mixer(x, w_in, conv_w, a_log, dt_bias, o_norm_w, sgu_ln_g, sgu_ln_b,
                 w_s, b_s, w_pa, w_pb, w_o):
    f32 = jnp.float32
    B, S, _ = x.shape
    proj = x @ w_in
    cuts = np.cumsum(IN_SIZES)[:-1].tolist()
    q, k, v, z, b_logit, a_logit, u, vg, ga, gb = jnp.split(proj, cuts, axis=-1)

    qkv = jax.nn.silu(causal_depthwise_conv(jnp.concatenate([q, k, v], -1), conv_w))
    q, k, v = jnp.split(qkv, [DN_QK_W, 2 * DN_QK_W], axis=-1)
    q = l2norm(q.reshape(B, S, DN_HEADS, DN_DK))
    k = l2norm(k.reshape(B, S, DN_HEADS, DN_DK))
    v = v.reshape(B, S, DN_HEADS, DN_DV)
    beta = jax.nn.sigmoid(b_logit.astype(f32))
    g = -jnp.exp(a_log.astype(f32)) * jax.nn.softplus(a_logit.astype(f32) + dt_bias)
    o = gated_delta_rule(q, k, v, g, beta)
    zf = z.reshape(B, S, DN_HEADS, DN_DV).astype(f32)
    o = o * lax.rsqrt(jnp.mean(o * o, -1, keepdims=True) + RMS_EPS) * o_norm_w * jax.nn.silu(zf)
    y_a = o.reshape(B, S, DN_V_W).astype(x.dtype)

    u = jax.nn.gelu(u, approximate=False)
    vg = layer_norm(jax.nn.gelu(vg, approximate=False), sgu_ln_g, sgu_ln_b)
    nb = S // SGU_BLOCK
    vb = vg.reshape(B, nb, SGU_BLOCK, SGU_GROUPS, SGU_GROUP_DIM)
    ch = jnp.arange(SGU_BLOCK) // CHUNK
    ws = jnp.where(ch[:, None] >= ch[None, :], w_s, 0.0)
    sp = jnp.einsum("gij,bnjgc->bnigc", ws, vb) + b_s.T[None, None, :, :, None]
    y_b = u * sp.reshape(B, S, SGU_W)

    m = jax.nn.sigmoid(ga) * (y_a @ w_pa) + jax.nn.sigmoid(gb) * (y_b @ w_pb)
    return m @ w_o


def swiglu(x, w_gate, w_up, w_down):
    return (jax.nn.silu(x @ w_gate) * (x @ w_up)) @ w_down


def setup_inputs(seed: int = 0) -> dict:
    key = jax.random.key(seed)
    ks = jax.random.split(key, 24)
    f32 = jnp.float32
    L = DEPTH
    nrm = lambda k, shape, s: jax.random.normal(k, shape, f32) * s
    dt = jnp.exp(jax.random.uniform(ks[4], (L, DN_HEADS), f32,
                                    math.log(1e-3), math.log(1e-1)))
    return {
        "x": jax.random.normal(ks[0], (BATCH, SEQ, D_MODEL), f32),
        "w_in": nrm(ks[1], (L, D_MODEL, N_IN), D_MODEL ** -0.5),
        "conv_w": nrm(ks[2], (L, CONV_K, 2 * DN_QK_W + DN_V_W), CONV_K ** -0.5),
        "a_log": jnp.log(jax.random.uniform(ks[3], (L, DN_HEADS), f32, 1.0, 16.0)),
        "dt_bias": dt + jnp.log(-jnp.expm1(-dt)),
        "o_norm_w": 1.0 + nrm(ks[5], (L, DN_DV), 0.02),
        "sgu_ln_g": 1.0 + nrm(ks[6], (L, SGU_W), 0.02),
        "sgu_ln_b": nrm(ks[7], (L, SGU_W), 0.02),
        "w_s": nrm(ks[8], (L, SGU_GROUPS, SGU_BLOCK, SGU_BLOCK), SGU_BLOCK ** -0.5),
        "b_s": 1.0 + nrm(ks[9], (L, SGU_GROUPS, SGU_BLOCK), 0.02),
        "w_pa": nrm(ks[10], (L, DN_V_W, D_MODEL), DN_V_W ** -0.5 * DEEPNORM_BETA),
        "w_pb": nrm(ks[11], (L, SGU_W, D_MODEL), SGU_W ** -0.5 * DEEPNORM_BETA),
        "w_o": nrm(ks[12], (L, D_MODEL, D_MODEL), D_MODEL ** -0.5 * DEEPNORM_BETA),
        "ln1_g": 1.0 + nrm(ks[13], (L, D_MODEL), 0.02),
        "ln1_b": nrm(ks[14], (L, D_MODEL), 0.02),
        "w_ffn_gate": nrm(ks[15], (L, D_MODEL, FFN_HIDDEN), D_MODEL ** -0.5),
        "w_ffn_up": nrm(ks[16], (L, D_MODEL, FFN_HIDDEN), D_MODEL ** -0.5),
        "w_ffn_down": nrm(ks[17], (L, FFN_HIDDEN, D_MODEL), FFN_HIDDEN ** -0.5 * DEEPNORM_BETA),
        "ln2_g": 1.0 + nrm(ks[18], (L, D_MODEL), 0.02),
        "ln2_b": nrm(ks[19], (L, D_MODEL), 0.02),
    }


def reference(x, w_in, conv_w, a_log, dt_bias, o_norm_w, sgu_ln_g, sgu_ln_b, w_s, b_s,
              w_pa, w_pb, w_o, ln1_g, ln1_b, w_ffn_gate, w_ffn_up, w_ffn_down,
              ln2_g, ln2_b):
    for l in range(DEPTH):
        mix = hybrid_mixer(x, w_in[l], conv_w[l], a_log[l], dt_bias[l], o_norm_w[l],
                           sgu_ln_g[l], sgu_ln_b[l], w_s[l], b_s[l],
                           w_pa[l], w_pb[l], w_o[l])
        x = layer_norm(DEEPNORM_ALPHA * x + mix, ln1_g[l], ln1_b[l])
        ffn = swiglu(x, w_ffn_gate[l], w_ffn_up[l], w_ffn_down[l])
        x = layer_norm(DEEPNORM_ALPHA * x + ffn, ln2_g[l], ln2_b[l])
    return x
```

```python
import functools
import math

import jax
import jax.numpy as jnp
from jax import lax
from jax.experimental import pallas as pl
from jax.experimental.pallas import tpu as pltpu

F32 = jnp.float32
BF16 = jnp.bfloat16

CHUNK = 64
DN_HEADS = 8
DN_DK = 128
DN_DV = 128
CONV_K = 4
SGU_BLOCK = 128
SGU_GROUPS = 8
SGU_GROUP_DIM = 128
LN_EPS = 1e-5
RMS_EPS = 1e-6

LANES = 128
SUBLANES = 8
VMEM_LIMIT = 56 * 1024 * 1024

SEC_Q, SEC_K, SEC_V, SEC_Z, SEC_U, SEC_VG, SEC_GA, SEC_GB = range(8)


def _layer_norm(h, g, b):
    mu = jnp.mean(h, -1, keepdims=True)
    d = h - mu
    var = jnp.mean(d * d, -1, keepdims=True)
    return d * lax.rsqrt(var + LN_EPS) * g + b


def _gelu(x):
    return 0.5 * x * (1.0 + lax.erf(x * (1.0 / math.sqrt(2.0))))


def _silu(x):
    return x * jax.nn.sigmoid(x)


def _dot(a, b):
    return jnp.dot(a.astype(BF16), b.astype(BF16), preferred_element_type=F32)


def _dot_nt(a, b):
    return lax.dot_general(a.astype(BF16), b.astype(BF16), (((1,), (1,)), ((), ())),
                           preferred_element_type=F32)


def _dot_tn(a, b):
    return lax.dot_general(a.astype(BF16), b.astype(BF16), (((0,), (0,)), ((), ())),
                           preferred_element_type=F32)


def _matmul_kernel(x_ref, w_ref, o_ref):
    o_ref[...] = jnp.dot(x_ref[...], w_ref[...],
                         preferred_element_type=F32).astype(o_ref.dtype)


def _in_proj(xb, w, tm, tn):
    t, d = xb.shape
    n = w.shape[1]
    return pl.pallas_call(
        _matmul_kernel,
        grid=(n // tn, t // tm),
        in_specs=[pl.BlockSpec((tm, d), lambda j, i: (i, 0)),
                  pl.BlockSpec((d, tn), lambda j, i: (0, j))],
        out_specs=pl.BlockSpec((tm, tn), lambda j, i: (i, j)),
        out_shape=jax.ShapeDtypeStruct((t, n), F32),
        compiler_params=pltpu.CompilerParams(
            dimension_semantics=("arbitrary", "arbitrary"),
            vmem_limit_bytes=VMEM_LIMIT),
        name="in_proj",
    )(xb, w)


def _deltanet_kernel(qkvz_ref, halo_ref, small_ref, convw_ref, apar_ref, onw_ref,
                     y_ref, s_ref, q_s, k_s, v_s, g_s, b_s, *, ts, d):
    t = pl.program_id(1)
    heads = DN_HEADS

    @pl.when(t == 0)
    def _():
        s_ref[...] = jnp.zeros_like(s_ref)

    not_first = (t > 0).astype(F32)
    row8 = lax.broadcasted_iota(jnp.int32, (SUBLANES, LANES), 0)
    for sec, dst in ((SEC_Q, q_s), (SEC_K, k_s), (SEC_V, v_s)):
        for h in range(heads):
            c0 = sec * d + h * LANES
            x = qkvz_ref[:, c0:c0 + LANES]
            hl = halo_ref[:, c0:c0 + LANES] * not_first
            w = convw_ref[:, c0:c0 + LANES]
            acc = x * w[CONV_K - 1:CONV_K]
            top = x[0:SUBLANES] * w[CONV_K - 1:CONV_K]
            for k in range(1, CONV_K):
                wk = w[CONV_K - 1 - k:CONV_K - k]
                r = pltpu.roll(x, k, axis=0)
                acc = acc + r * wk
                hr = pltpu.roll(hl, k, axis=0)
                top = top + jnp.where(row8 < k, hr, r[0:SUBLANES]) * wk
            y = _silu(jnp.concatenate([top, acc[SUBLANES:]], axis=0))
            if sec != SEC_V:
                y = y * lax.rsqrt(jnp.sum(y * y, -1, keepdims=True) + RMS_EPS)
            if sec == SEC_Q:
                y = y * (DN_DK ** -0.5)
            dst[:, h * LANES:(h + 1) * LANES] = y

    sm = small_ref[...]
    b_s[...] = jax.nn.sigmoid(sm)
    g_s[...] = -jnp.exp(apar_ref[0:1, :]) * jax.nn.softplus(sm + apar_ref[1:2, :])

    ri = lax.broadcasted_iota(jnp.int32, (CHUNK, CHUNK), 0)
    ci = lax.broadcasted_iota(jnp.int32, (CHUNK, CHUNK), 1)
    tril = ri >= ci
    strict = ri > ci
    tril_f = tril.astype(F32)
    eye = (ri == ci).astype(F32)
    onw = onw_ref[...]

    def chunk_body(c, carry):
        r0 = pl.multiple_of(c * CHUNK, CHUNK)
        rows = pl.ds(r0, CHUNK)
        gcum = jnp.dot(tril_f, g_s[rows, :], preferred_element_type=F32,
                       precision=lax.Precision.HIGHEST)
        gcum_t = gcum.T
        beta = b_s[rows, :]
        for h in range(heads):
            hs = slice(h * LANES, (h + 1) * LANES)
            gi = gcum[:, heads + h:heads + h + 1]
            gj = gcum_t[heads + h:heads + h + 1, :]
            glast = gcum[CHUNK - 1:CHUNK, heads + h:heads + h + 1]
            dec = jnp.where(tril, jnp.exp(jnp.minimum(gi - gj, 0.0)), 0.0)
            bcol = beta[:, h:h + 1]
            qh = q_s[rows, hs]
            kh = k_s[rows, hs]
            vh = v_s[rows, hs]
            kb = kh * bcol
            eg = jnp.exp(gi)
            a = _dot_nt(jnp.concatenate([kb, qh], axis=0), kh)
            lmat = jnp.where(strict, a[:CHUNK] * dec, 0.0)
            attn = a[CHUNK:] * dec
            tinv = eye - lmat
            pw = lmat
            for _ in range(5):
                pw = _dot(pw, pw)
                tinv = tinv + _dot(tinv, pw)
            uw = _dot(tinv, jnp.concatenate([vh * bcol, kb * eg], axis=1))
            u_c = uw[:, :DN_DV]
            w_c = uw[:, DN_DV:]
            qg = qh * eg
            kd = kh * jnp.exp(glast - gi)
            s_old = s_ref[h]
            sw = _dot(jnp.concatenate([w_c, qg], axis=0), s_old)
            v_new = u_c - sw[:CHUNK]
            o = sw[CHUNK:] + _dot(attn, v_new)
            s_ref[h] = s_old * jnp.exp(glast) + _dot_tn(kd, v_new)
            z = qkvz_ref[rows, SEC_Z * d + h * LANES:SEC_Z * d + (h + 1) * LANES]
            o = o * lax.rsqrt(jnp.mean(o * o, -1, keepdims=True) + RMS_EPS) * onw * _silu(z)
            y_ref[rows, hs] = o.astype(y_ref.dtype)
        return carry

    lax.fori_loop(0, ts // CHUNK, chunk_body, 0)


def _deltanet(proj, convw, apar, onw, batch, seq, d, ts):
    t_total = proj.shape[0]
    nt = seq // ts
    small_blk = 8 * d // LANES
    kern = functools.partial(_deltanet_kernel, ts=ts, d=d)
    return pl.pallas_call(
        kern,
        grid=(batch, nt),
        in_specs=[
            pl.BlockSpec((ts, 4 * d), lambda b, t: (b * nt + t, 0)),
            pl.BlockSpec((SUBLANES, 4 * d),
                         lambda b, t: (jnp.maximum((b * nt + t) * (ts // SUBLANES) - 1, 0), 0)),
            pl.BlockSpec((ts, LANES), lambda b, t: (b * nt + t, small_blk)),
            pl.BlockSpec((CONV_K, 3 * d), lambda b, t: (0, 0)),
            pl.BlockSpec((2, LANES), lambda b, t: (0, 0)),
            pl.BlockSpec((1, LANES), lambda b, t: (0, 0)),
        ],
        out_specs=pl.BlockSpec((ts, d), lambda b, t: (b * nt + t, 0)),
        out_shape=jax.ShapeDtypeStruct((t_total, d), BF16),
        scratch_shapes=[
            pltpu.VMEM((DN_HEADS, DN_DK, DN_DV), F32),
            pltpu.VMEM((ts, d), F32),
            pltpu.VMEM((ts, d), F32),
            pltpu.VMEM((ts, d), F32),
            pltpu.VMEM((ts, LANES), F32),
            pltpu.VMEM((ts, LANES), F32),
        ],
        compiler_params=pltpu.CompilerParams(
            dimension_semantics=("arbitrary", "arbitrary"),
            vmem_limit_bytes=VMEM_LIMIT),
        name="deltanet",
    )(proj, proj, proj, convw, apar, onw)


def _sgu_kernel(u_ref, vg_ref, ws_ref, bst_ref, lng_ref, lnb_ref, y_ref, *, nblk):
    ri = lax.broadcasted_iota(jnp.int32, (SGU_BLOCK, SGU_BLOCK), 0) // CHUNK
    ci = lax.broadcasted_iota(jnp.int32, (SGU_BLOCK, SGU_BLOCK), 1) // CHUNK
    causal = ri >= ci
    lng = lng_ref[...]
    lnb = lnb_ref[...]
    for n in range(nblk):
        rows = slice(n * SGU_BLOCK, (n + 1) * SGU_BLOCK)
        vn = _layer_norm(_gelu(vg_ref[rows, :]), lng, lnb)
        for g in range(SGU_GROUPS):
            cs = slice(g * SGU_GROUP_DIM, (g + 1) * SGU_GROUP_DIM)
            ws = jnp.where(causal, ws_ref[g], 0.0)
            sp = _dot(ws, vn[:, cs]) + bst_ref[:, g:g + 1]
            y_ref[rows, cs] = (_gelu(u_ref[rows, cs]) * sp).astype(y_ref.dtype)


def _sgu(proj, w_s, b_s_t, ln_g, ln_b, d, nblk):
    t_total = proj.shape[0]
    tm = nblk * SGU_BLOCK
    kern = functools.partial(_sgu_kernel, nblk=nblk)
    return pl.pallas_call(
        kern,
        grid=(t_total // tm,),
        in_specs=[
            pl.BlockSpec((tm, d), lambda i: (i, SEC_U)),
            pl.BlockSpec((tm, d), lambda i: (i, SEC_VG)),
            pl.BlockSpec((SGU_GROUPS, SGU_BLOCK, SGU_BLOCK), lambda i: (0, 0, 0)),
            pl.BlockSpec((SGU_BLOCK, SGU_GROUPS), lambda i: (0, 0)),
            pl.BlockSpec((1, d), lambda i: (0, 0)),
            pl.BlockSpec((1, d), lambda i: (0, 0)),
        ],
        out_specs=pl.BlockSpec((tm, d), lambda i: (i, 0)),
        out_shape=jax.ShapeDtypeStruct((t_total, d), BF16),
        compiler_params=pltpu.CompilerParams(
            dimension_semantics=("arbitrary",), vmem_limit_bytes=VMEM_LIMIT),
        name="sgu",
    )(proj, proj, w_s, b_s_t, ln_g, ln_b)


def _merge_kernel(ya_ref, yb_ref, ga_ref, gb_ref, x_ref, wpa_ref, wpb_ref, wo_ref,
                  g_ref, b_ref, o_ref, ob_ref, *, alpha):
    pa = jnp.dot(ya_ref[...], wpa_ref[...], preferred_element_type=F32)
    pb = jnp.dot(yb_ref[...], wpb_ref[...], preferred_element_type=F32)
    m = jax.nn.sigmoid(ga_ref[...]) * pa + jax.nn.sigmoid(gb_ref[...]) * pb
    mix = jnp.dot(m.astype(BF16), wo_ref[...], preferred_element_type=F32)
    out = _layer_norm(alpha * x_ref[...] + mix, g_ref[...], b_ref[...])
    o_ref[...] = out
    ob_ref[...] = out.astype(BF16)


def _resident(shape):
    return pl.BlockSpec(shape, lambda i: (0,) * len(shape), pipeline_mode=pl.Buffered(1))


def _merge(ya, yb, proj, x, wpa, wpb, wo, ln_g, ln_b, alpha, tm):
    t_total, d = x.shape
    kern = functools.partial(_merge_kernel, alpha=alpha)
    row = pl.BlockSpec((tm, d), lambda i: (i, 0))
    return pl.pallas_call(
        kern,
        grid=(t_total // tm,),
        in_specs=[row, row,
                  pl.BlockSpec((tm, d), lambda i: (i, SEC_GA)),
                  pl.BlockSpec((tm, d), lambda i: (i, SEC_GB)),
                  row,
                  _resident((d, d)), _resident((d, d)), _resident((d, d)),
                  _resident((1, d)), _resident((1, d))],
        out_specs=[row, row],
        out_shape=[jax.ShapeDtypeStruct((t_total, d), F32),
                   jax.ShapeDtypeStruct((t_total, d), BF16)],
        compiler_params=pltpu.CompilerParams(
            dimension_semantics=("arbitrary",), vmem_limit_bytes=VMEM_LIMIT),
        name="merge",
    )(ya, yb, proj, proj, x, wpa, wpb, wo, ln_g, ln_b)


def _ffn_kernel(x_ref, xb_ref, wg_ref, wu_ref, wd_ref, g_ref, b_ref, o_ref, ob_ref, *, alpha):
    xb = xb_ref[...]
    gate = jnp.dot(xb, wg_ref[...], preferred_element_type=F32)
    up = jnp.dot(xb, wu_ref[...], preferred_element_type=F32)
    hid = (_silu(gate) * up).astype(BF16)
    ffn = jnp.dot(hid, wd_ref[...], preferred_element_type=F32)
    out = _layer_norm(alpha * x_ref[...] + ffn, g_ref[...], b_ref[...])
    o_ref[...] = out
    ob_ref[...] = out.astype(BF16)


def _ffn(x, xb, wg, wu, wd, ln_g, ln_b, alpha, tm):
    t_total, d = x.shape
    hdim = wg.shape[1]
    kern = functools.partial(_ffn_kernel, alpha=alpha)
    row = pl.BlockSpec((tm, d), lambda i: (i, 0))
    return pl.pallas_call(
        kern,
        grid=(t_total // tm,),
        in_specs=[row, row,
                  _resident((d, hdim)), _resident((d, hdim)), _resident((hdim, d)),
                  _resident((1, d)), _resident((1, d))],
        out_specs=[row, row],
        out_shape=[jax.ShapeDtypeStruct((t_total, d), F32),
                   jax.ShapeDtypeStruct((t_total, d), BF16)],
        compiler_params=pltpu.CompilerParams(
            dimension_semantics=("arbitrary",), vmem_limit_bytes=VMEM_LIMIT),
        name="ffn",
    )(x, xb, wg, wu, wd, ln_g, ln_b)


def _pick_tile(n, want):
    tile = min(n, want)
    assert n % tile == 0, (n, tile)
    return tile


def kernel(x, w_in, conv_w, a_log, dt_bias, o_norm_w, sgu_ln_g, sgu_ln_b, w_s, b_s, w_pa, w_pb, w_o, ln1_g, ln1_b, w_ffn_gate, w_ffn_up, w_ffn_down, ln2_g, ln2_b):
    batch, seq, d = x.shape
    depth = w_in.shape[0]
    assert d == DN_HEADS * DN_DK == SGU_GROUPS * SGU_GROUP_DIM
    assert seq % SGU_BLOCK == 0
    alpha = (2 * depth) ** 0.25
    t_total = batch * seq

    xf = x.reshape(t_total, d)
    xb = xf.astype(BF16)
    pad = jnp.zeros((d, LANES - 2 * DN_HEADS), F32)
    lane_pad = jnp.zeros((DN_HEADS,), F32)
    for l in range(depth):
        wl = w_in[l]
        w_main = jnp.concatenate([wl[:, :4 * d], wl[:, 4 * d + 2 * DN_HEADS:]], axis=1)
        w_small = jnp.concatenate([wl[:, 4 * d:4 * d + 2 * DN_HEADS], pad], axis=1)
        w_all = jnp.concatenate([w_main, w_small], axis=1).astype(BF16)
        n_all = w_all.shape[1]
        proj = _in_proj(xb, w_all, _pick_tile(t_total, 1024), n_all // 5)

        apar = jnp.stack([
            jnp.concatenate([lane_pad, a_log[l], jnp.zeros((LANES - 2 * DN_HEADS,), F32)]),
            jnp.concatenate([lane_pad, dt_bias[l], jnp.zeros((LANES - 2 * DN_HEADS,), F32)]),
        ])
        ya = _deltanet(proj, conv_w[l], apar, o_norm_w[l].reshape(1, DN_DV),
                       batch, seq, d, _pick_tile(seq, 256))
        yb = _sgu(proj, w_s[l], b_s[l].T, sgu_ln_g[l].reshape(1, d), sgu_ln_b[l].reshape(1, d),
                  d, _pick_tile(t_total // SGU_BLOCK, 4))
        xf, xb = _merge(ya, yb, proj, xf, w_pa[l].astype(BF16), w_pb[l].astype(BF16),
                        w_o[l].astype(BF16), ln1_g[l].reshape(1, d), ln1_b[l].reshape(1, d),
                        alpha, _pick_tile(t_total, 512))
        xf, xb = _ffn(xf, xb, w_ffn_gate[l].astype(BF16), w_ffn_up[l].astype(BF16),
                      w_ffn_down[l].astype(BF16), ln2_g[l].reshape(1, d), ln2_b[l].reshape(1, d),
                      alpha, _pick_tile(t_total, 256))
    return xf.reshape(batch, seq, d)
```

```python
import functools
import math

import jax
import jax.numpy as jnp
from jax import lax
from jax.experimental import pallas as pl
from jax.experimental.pallas import tpu as pltpu

F32 = jnp.float32
BF16 = jnp.bfloat16

CHUNK = 64
DN_HEADS = 8
DN_DK = 128
DN_DV = 128
CONV_K = 4
SGU_BLOCK = 128
SGU_GROUPS = 8
SGU_GROUP_DIM = 128
LN_EPS = 1e-5
RMS_EPS = 1e-6

LANES = 128
SUBLANES = 8
VMEM_LIMIT = 56 * 1024 * 1024

SEC_Q, SEC_K, SEC_V, SEC_Z, SEC_U, SEC_VG, SEC_GA, SEC_GB = range(8)


def _layer_norm(h, g, b):
    mu = jnp.mean(h, -1, keepdims=True)
    d = h - mu
    var = jnp.mean(d * d, -1, keepdims=True)
    return d * lax.rsqrt(var + LN_EPS) * g + b


def _gelu(x):
    return 0.5 * x * (1.0 + lax.erf(x * (1.0 / math.sqrt(2.0))))


def _silu(x):
    return x * jax.nn.sigmoid(x)


def _dot(a, b):
    return jnp.dot(a.astype(BF16), b.astype(BF16), preferred_element_type=F32)


def _dot_nt(a, b):
    return lax.dot_general(a.astype(BF16), b.astype(BF16), (((1,), (1,)), ((), ())),
                           preferred_element_type=F32)


def _dot_tn(a, b):
    return lax.dot_general(a.astype(BF16), b.astype(BF16), (((0,), (0,)), ((), ())),
                           preferred_element_type=F32)


def _matmul_kernel(x_ref, w_ref, o_ref):
    o_ref[...] = jnp.dot(x_ref[...], w_ref[...],
                         preferred_element_type=F32).astype(o_ref.dtype)


def _in_proj(xb, w, tm, tn):
    t, d = xb.shape
    n = w.shape[1]
    return pl.pallas_call(
        _matmul_kernel,
        grid=(n // tn, t // tm),
        in_specs=[pl.BlockSpec((tm, d), lambda j, i: (i, 0)),
                  pl.BlockSpec((d, tn), lambda j, i: (0, j))],
        out_specs=pl.BlockSpec((tm, tn), lambda j, i: (i, j)),
        out_shape=jax.ShapeDtypeStruct((t, n), F32),
        compiler_params=pltpu.CompilerParams(
            dimension_semantics=("arbitrary", "arbitrary"),
            vmem_limit_bytes=VMEM_LIMIT),
        name="in_proj",
    )(xb, w)


def _deltanet_kernel(qkvz_ref, halo_ref, small_ref, convw_ref, apar_ref, onw_ref,
                     y_ref, s_ref, q_s, k_s, v_s, g_s, b_s, u_s, wq_s, at_s, kdt_s, el_s,
                     *, ts, d):
    t = pl.program_id(1)
    heads = DN_HEADS

    @pl.when(t == 0)
    def _():
        s_ref[...] = jnp.zeros_like(s_ref)

    not_first = (t > 0).astype(F32)
    row8 = lax.broadcasted_iota(jnp.int32, (SUBLANES, LANES), 0)
    for sec, dst in ((SEC_Q, q_s), (SEC_K, k_s), (SEC_V, v_s)):
        for h in range(heads):
            c0 = sec * d + h * LANES
            x = qkvz_ref[:, c0:c0 + LANES]
            hl = halo_ref[:, c0:c0 + LANES] * not_first
            w = convw_ref[:, c0:c0 + LANES]
            acc = x * w[CONV_K - 1:CONV_K]
            top = x[0:SUBLANES] * w[CONV_K - 1:CONV_K]
            for k in range(1, CONV_K):
                wk = w[CONV_K - 1 - k:CONV_K - k]
                r = pltpu.roll(x, k, axis=0)
                acc = acc + r * wk
                hroll = pltpu.roll(hl, k, axis=0)
                top = top + jnp.where(row8 < k, hroll, r[0:SUBLANES]) * wk
            y = _silu(jnp.concatenate([top, acc[SUBLANES:]], axis=0))
            if sec != SEC_V:
                y = y * lax.rsqrt(jnp.sum(y * y, -1, keepdims=True) + RMS_EPS)
            if sec == SEC_Q:
                y = y * (DN_DK ** -0.5)
            dst[:, h * LANES:(h + 1) * LANES] = y

    sm = small_ref[...]
    b_s[...] = jax.nn.sigmoid(sm)
    g_s[...] = -jnp.exp(apar_ref[0:1, :]) * jax.nn.softplus(sm + apar_ref[1:2, :])

    ri = lax.broadcasted_iota(jnp.int32, (CHUNK, CHUNK), 0)
    ci = lax.broadcasted_iota(jnp.int32, (CHUNK, CHUNK), 1)
    tril = ri >= ci
    strict = ri > ci
    tril_f = tril.astype(F32)
    eye = (ri == ci).astype(F32)
    onw = onw_ref[...]

    hr = range(heads)
    hs = [slice(h * LANES, (h + 1) * LANES) for h in hr]

    def prep_chunk(c, carry):
        r0 = pl.multiple_of(c * CHUNK, CHUNK)
        rows = pl.ds(r0, CHUNK)
        gcum = jnp.dot(tril_f, g_s[rows, :], preferred_element_type=F32,
                       precision=lax.Precision.HIGHEST)
        gcum_t = gcum.T
        beta = b_s[rows, :]
        gi = [gcum[:, heads + h:heads + h + 1] for h in hr]
        gj = [gcum_t[heads + h:heads + h + 1, :] for h in hr]
        glast = [gcum[CHUNK - 1:CHUNK, heads + h:heads + h + 1] for h in hr]
        dec = [jnp.where(tril, jnp.exp(jnp.minimum(gi[h] - gj[h], 0.0)), 0.0) for h in hr]
        bcol = [beta[:, h:h + 1] for h in hr]
        eg = [jnp.exp(gi[h]) for h in hr]
        kh = [k_s[rows, hs[h]] for h in hr]
        kb = [kh[h] * bcol[h] for h in hr]
        a = [_dot_nt(jnp.concatenate([kb[h], q_s[rows, hs[h]]], axis=0), kh[h]) for h in hr]
        pw = [jnp.where(strict, a[h][:CHUNK] * dec[h], 0.0) for h in hr]
        for h in hr:
            at_s[c, h] = (a[h][CHUNK:] * dec[h]).astype(BF16)
        tinv = [eye - pw[h] for h in hr]
        for _ in range(5):
            pw = [_dot(pw[h], pw[h]) for h in hr]
            tinv = [tinv[h] + _dot(tinv[h], pw[h]) for h in hr]
        uw = [_dot(tinv[h], jnp.concatenate([v_s[rows, hs[h]] * bcol[h], kb[h] * eg[h]], axis=1))
              for h in hr]
        for h in hr:
            u_s[c, h] = uw[h][:, :DN_DV]
            wq_s[c, h] = jnp.concatenate([uw[h][:, DN_DV:], q_s[rows, hs[h]] * eg[h]],
                                         axis=0).astype(BF16)
            kd = kh[h] * jnp.exp(glast[h] - gi[h])
            kdt_s[c, h] = kd.T.astype(BF16)
            el_s[c, h:h + 1, :] = jnp.broadcast_to(jnp.exp(glast[h]), (1, LANES))
        return carry

    def scan_chunk(c, carry):
        r0 = pl.multiple_of(c * CHUNK, CHUNK)
        rows = pl.ds(r0, CHUNK)
        s_old = [s_ref[h] for h in hr]
        sw = [_dot(wq_s[c, h], s_old[h]) for h in hr]
        v_new = [u_s[c, h] - sw[h][:CHUNK] for h in hr]
        o = [sw[h][CHUNK:] + _dot(at_s[c, h], v_new[h]) for h in hr]
        for h in hr:
            s_ref[h] = s_old[h] * el_s[c, h:h + 1, :] + _dot(kdt_s[c, h], v_new[h])
        for h in hr:
            z = qkvz_ref[rows, SEC_Z * d + h * LANES:SEC_Z * d + (h + 1) * LANES]
            y = (o[h] * lax.rsqrt(jnp.mean(o[h] * o[h], -1, keepdims=True) + RMS_EPS)
                 * onw * _silu(z))
            y_ref[rows, hs[h]] = y.astype(y_ref.dtype)
        return carry

    lax.fori_loop(0, ts // CHUNK, prep_chunk, 0)
    lax.fori_loop(0, ts // CHUNK, scan_chunk, 0)


def _deltanet(proj, convw, apar, onw, batch, seq, d, ts):
    t_total = proj.shape[0]
    nt = seq // ts
    small_blk = 8 * d // LANES
    kern = functools.partial(_deltanet_kernel, ts=ts, d=d)
    return pl.pallas_call(
        kern,
        grid=(batch, nt),
        in_specs=[
            pl.BlockSpec((ts, 4 * d), lambda b, t: (b * nt + t, 0)),
            pl.BlockSpec((SUBLANES, 4 * d),
                         lambda b, t: (jnp.maximum((b * nt + t) * (ts // SUBLANES) - 1, 0), 0)),
            pl.BlockSpec((ts, LANES), lambda b, t: (b * nt + t, small_blk)),
            pl.BlockSpec((CONV_K, 3 * d), lambda b, t: (0, 0)),
            pl.BlockSpec((2, LANES), lambda b, t: (0, 0)),
            pl.BlockSpec((1, LANES), lambda b, t: (0, 0)),
        ],
        out_specs=pl.BlockSpec((ts, d), lambda b, t: (b * nt + t, 0)),
        out_shape=jax.ShapeDtypeStruct((t_total, d), BF16),
        scratch_shapes=[
            pltpu.VMEM((DN_HEADS, DN_DK, DN_DV), F32),
            pltpu.VMEM((ts, d), F32),
            pltpu.VMEM((ts, d), F32),
            pltpu.VMEM((ts, d), F32),
            pltpu.VMEM((ts, LANES), F32),
            pltpu.VMEM((ts, LANES), F32),
            pltpu.VMEM((ts // CHUNK, DN_HEADS, CHUNK, DN_DV), F32),
            pltpu.VMEM((ts // CHUNK, DN_HEADS, 2 * CHUNK, DN_DK), BF16),
            pltpu.VMEM((ts // CHUNK, DN_HEADS, CHUNK, CHUNK), BF16),
            pltpu.VMEM((ts // CHUNK, DN_HEADS, DN_DK, CHUNK), BF16),
            pltpu.VMEM((ts // CHUNK, DN_HEADS, LANES), F32),
        ],
        compiler_params=pltpu.CompilerParams(
            dimension_semantics=("arbitrary", "arbitrary"),
            vmem_limit_bytes=VMEM_LIMIT),
        name="deltanet",
    )(proj, proj, proj, convw, apar, onw)


def _sgu_kernel(u_ref, vg_ref, ws_ref, bst_ref, lng_ref, lnb_ref, y_ref, *, nblk):
    ri = lax.broadcasted_iota(jnp.int32, (SGU_BLOCK, SGU_BLOCK), 0) // CHUNK
    ci = lax.broadcasted_iota(jnp.int32, (SGU_BLOCK, SGU_BLOCK), 1) // CHUNK
    causal = ri >= ci
    lng = lng_ref[...]
    lnb = lnb_ref[...]
    for n in range(nblk):
        rows = slice(n * SGU_BLOCK, (n + 1) * SGU_BLOCK)
        vn = _layer_norm(_gelu(vg_ref[rows, :]), lng, lnb)
        for g in range(SGU_GROUPS):
            cs = slice(g * SGU_GROUP_DIM, (g + 1) * SGU_GROUP_DIM)
            ws = jnp.where(causal, ws_ref[g], 0.0)
            sp = _dot(ws, vn[:, cs]) + bst_ref[:, g:g + 1]
            y_ref[rows, cs] = (_gelu(u_ref[rows, cs]) * sp).astype(y_ref.dtype)


def _sgu(proj, w_s, b_s_t, ln_g, ln_b, d, nblk):
    t_total = proj.shape[0]
    tm = nblk * SGU_BLOCK
    kern = functools.partial(_sgu_kernel, nblk=nblk)
    return pl.pallas_call(
        kern,
        grid=(t_total // tm,),
        in_specs=[
            pl.BlockSpec((tm, d), lambda i: (i, SEC_U)),
            pl.BlockSpec((tm, d), lambda i: (i, SEC_VG)),
            pl.BlockSpec((SGU_GROUPS, SGU_BLOCK, SGU_BLOCK), lambda i: (0, 0, 0)),
            pl.BlockSpec((SGU_BLOCK, SGU_GROUPS), lambda i: (0, 0)),
            pl.BlockSpec((1, d), lambda i: (0, 0)),
            pl.BlockSpec((1, d), lambda i: (0, 0)),
        ],
        out_specs=pl.BlockSpec((tm, d), lambda i: (i, 0)),
        out_shape=jax.ShapeDtypeStruct((t_total, d), BF16),
        compiler_params=pltpu.CompilerParams(
            dimension_semantics=("arbitrary",), vmem_limit_bytes=VMEM_LIMIT),
        name="sgu",
    )(proj, proj, w_s, b_s_t, ln_g, ln_b)


def _merge_kernel(ya_ref, yb_ref, ga_ref, gb_ref, x_ref, wpa_ref, wpb_ref, wo_ref,
                  g_ref, b_ref, o_ref, ob_ref, *, alpha):
    pa = jnp.dot(ya_ref[...], wpa_ref[...], preferred_element_type=F32)
    pb = jnp.dot(yb_ref[...], wpb_ref[...], preferred_element_type=F32)
    m = jax.nn.sigmoid(ga_ref[...]) * pa + jax.nn.sigmoid(gb_ref[...]) * pb
    mix = jnp.dot(m.astype(BF16), wo_ref[...], preferred_element_type=F32)
    out = _layer_norm(alpha * x_ref[...] + mix, g_ref[...], b_ref[...])
    o_ref[...] = out
    ob_ref[...] = out.astype(BF16)


def _resident(shape):
    return pl.BlockSpec(shape, lambda i: (0,) * len(shape), pipeline_mode=pl.Buffered(1))


def _merge(ya, yb, proj, x, wpa, wpb, wo, ln_g, ln_b, alpha, tm):
    t_total, d = x.shape
    kern = functools.partial(_merge_kernel, alpha=alpha)
    row = pl.BlockSpec((tm, d), lambda i: (i, 0))
    return pl.pallas_call(
        kern,
        grid=(t_total // tm,),
        in_specs=[row, row,
                  pl.BlockSpec((tm, d), lambda i: (i, SEC_GA)),
                  pl.BlockSpec((tm, d), lambda i: (i, SEC_GB)),
                  row,
                  _resident((d, d)), _resident((d, d)), _resident((d, d)),
                  _resident((1, d)), _resident((1, d))],
        out_specs=[row, row],
        out_shape=[jax.ShapeDtypeStruct((t_total, d), F32),
                   jax.ShapeDtypeStruct((t_total, d), BF16)],
        compiler_params=pltpu.CompilerParams(
            dimension_semantics=("arbitrary",), vmem_limit_bytes=VMEM_LIMIT),
        name="merge",
    )(ya, yb, proj, proj, x, wpa, wpb, wo, ln_g, ln_b)


def _ffn_kernel(x_ref, xb_ref, wg_ref, wu_ref, wd_ref, g_ref, b_ref, o_ref, ob_ref, *, alpha):
    xb = xb_ref[...]
    gate = jnp.dot(xb, wg_ref[...], preferred_element_type=F32)
    up = jnp.dot(xb, wu_ref[...], preferred_element_type=F32)
    hid = (_silu(gate) * up).astype(BF16)
    ffn = jnp.dot(hid, wd_ref[...], preferred_element_type=F32)
    out = _layer_norm(alpha * x_ref[...] + ffn, g_ref[...], b_ref[...])
    o_ref[...] = out
    ob_ref[...] = out.astype(BF16)


def _ffn(x, xb, wg, wu, wd, ln_g, ln_b, alpha, tm):
    t_total, d = x.shape
    hdim = wg.shape[1]
    kern = functools.partial(_ffn_kernel, alpha=alpha)
    row = pl.BlockSpec((tm, d), lambda i: (i, 0))
    return pl.pallas_call(
        kern,
        grid=(t_total // tm,),
        in_specs=[row, row,
                  _resident((d, hdim)), _resident((d, hdim)), _resident((hdim, d)),
                  _resident((1, d)), _resident((1, d))],
        out_specs=[row, row],
        out_shape=[jax.ShapeDtypeStruct((t_total, d), F32),
                   jax.ShapeDtypeStruct((t_total, d), BF16)],
        compiler_params=pltpu.CompilerParams(
            dimension_semantics=("arbitrary",), vmem_limit_bytes=VMEM_LIMIT),
        name="ffn",
    )(x, xb, wg, wu, wd, ln_g, ln_b)


def _pick_tile(n, want):
    tile = min(n, want)
    assert n % tile == 0, (n, tile)
    return tile


def kernel(x, w_in, conv_w, a_log, dt_bias, o_norm_w, sgu_ln_g, sgu_ln_b, w_s, b_s, w_pa, w_pb, w_o, ln1_g, ln1_b, w_ffn_gate, w_ffn_up, w_ffn_down, ln2_g, ln2_b):
    batch, seq, d = x.shape
    depth = w_in.shape[0]
    assert d == DN_HEADS * DN_DK == SGU_GROUPS * SGU_GROUP_DIM
    assert seq % SGU_BLOCK == 0
    alpha = (2 * depth) ** 0.25
    t_total = batch * seq

    xf = x.reshape(t_total, d)
    xb = xf.astype(BF16)
    pad = jnp.zeros((d, LANES - 2 * DN_HEADS), F32)
    lane_pad = jnp.zeros((DN_HEADS,), F32)
    for l in range(depth):
        wl = w_in[l]
        w_main = jnp.concatenate([wl[:, :4 * d], wl[:, 4 * d + 2 * DN_HEADS:]], axis=1)
        w_small = jnp.concatenate([wl[:, 4 * d:4 * d + 2 * DN_HEADS], pad], axis=1)
        w_all = jnp.concatenate([w_main, w_small], axis=1).astype(BF16)
        n_all = w_all.shape[1]
        proj = _in_proj(xb, w_all, _pick_tile(t_total, 1024), n_all // 5)

        apar = jnp.stack([
            jnp.concatenate([lane_pad, a_log[l], jnp.zeros((LANES - 2 * DN_HEADS,), F32)]),
            jnp.concatenate([lane_pad, dt_bias[l], jnp.zeros((LANES - 2 * DN_HEADS,), F32)]),
        ])
        ya = _deltanet(proj, conv_w[l], apar, o_norm_w[l].reshape(1, DN_DV),
                       batch, seq, d, _pick_tile(seq, 256))
        yb = _sgu(proj, w_s[l], b_s[l].T, sgu_ln_g[l].reshape(1, d), sgu_ln_b[l].reshape(1, d),
                  d, _pick_tile(t_total // SGU_BLOCK, 4))
        xf, xb = _merge(ya, yb, proj, xf, w_pa[l].astype(BF16), w_pb[l].astype(BF16),
                        w_o[l].astype(BF16), ln1_g[l].reshape(1, d), ln1_b[l].reshape(1, d),
                        alpha, _pick_tile(t_total, 512))
        xf, xb = _ffn(xf, xb, w_ffn_gate[l].astype(BF16), w_ffn_up[l].astype(BF16),
                      w_ffn_down[l].astype(BF16), ln2_g[l].reshape(1, d), ln2_b[l].reshape(1, d),
                      alpha, _pick_tile(t_total, 256))
    return xf.reshape(batch, seq, d)
```

```python
import functools
import math

import jax
import jax.numpy as jnp
from jax import lax
from jax.experimental import pallas as pl
from jax.experimental.pallas import tpu as pltpu

F32 = jnp.float32
BF16 = jnp.bfloat16

CHUNK = 64
DN_HEADS = 8
DN_DK = 128
DN_DV = 128
CONV_K = 4
SGU_BLOCK = 128
SGU_GROUPS = 8
SGU_GROUP_DIM = 128
LN_EPS = 1e-5
RMS_EPS = 1e-6
CONV_BUFS = 4

LANES = 128
SUBLANES = 8
VMEM_LIMIT = 56 * 1024 * 1024

SEC_Q, SEC_K, SEC_V, SEC_Z, SEC_U, SEC_VG, SEC_GA, SEC_GB = range(8)


def _layer_norm(h, g, b):
    mu = jnp.mean(h, -1, keepdims=True)
    d = h - mu
    var = jnp.mean(d * d, -1, keepdims=True)
    return d * lax.rsqrt(var + LN_EPS) * g + b


def _gelu(x):
    return 0.5 * x * (1.0 + lax.erf(x * (1.0 / math.sqrt(2.0))))


def _silu(x):
    return x * jax.nn.sigmoid(x)


def _dot(a, b):
    return jnp.dot(a.astype(BF16), b.astype(BF16), preferred_element_type=F32)


def _dot_nt(a, b):
    return lax.dot_general(a.astype(BF16), b.astype(BF16), (((1,), (1,)), ((), ())),
                           preferred_element_type=F32)


def _dot_tn(a, b):
    return lax.dot_general(a.astype(BF16), b.astype(BF16), (((0,), (0,)), ((), ())),
                           preferred_element_type=F32)


def _matmul_kernel(x_ref, w_ref, o_ref):
    o_ref[...] = jnp.dot(x_ref[...], w_ref[...],
                         preferred_element_type=F32).astype(o_ref.dtype)


def _in_proj(xb, w, tm, tn):
    t, d = xb.shape
    n = w.shape[1]
    return pl.pallas_call(
        _matmul_kernel,
        grid=(n // tn, t // tm),
        in_specs=[pl.BlockSpec((tm, d), lambda j, i: (i, 0)),
                  pl.BlockSpec((d, tn), lambda j, i: (0, j))],
        out_specs=pl.BlockSpec((tm, tn), lambda j, i: (i, j)),
        out_shape=jax.ShapeDtypeStruct((t, n), F32),
        compiler_params=pltpu.CompilerParams(
            dimension_semantics=("arbitrary", "arbitrary"),
            vmem_limit_bytes=VMEM_LIMIT),
        name="in_proj",
    )(xb, w)


def _deltanet_kernel(qkvz_ref, halo_ref, small_ref, convw_ref, apar_ref, onw_ref,
                     y_ref, s_ref, q_s, k_s, v_s, gb_s, bb_s, u_s, wq_s, at_s, kd_s, el_s, cbuf,
                     *, ts, d):
    t = pl.program_id(1)
    heads = DN_HEADS

    @pl.when(t == 0)
    def _():
        s_ref[...] = jnp.zeros_like(s_ref)

    not_first = (t > 0).astype(F32)
    n_slice = 0
    for sec, dst in ((SEC_Q, q_s), (SEC_K, k_s), (SEC_V, v_s)):
        for h in range(heads):
            c0 = sec * d + h * LANES
            cols = slice(c0, c0 + LANES)
            buf = cbuf.at[n_slice % CONV_BUFS]
            n_slice += 1
            buf[0:SUBLANES, :] = halo_ref[:, cols] * not_first
            buf[SUBLANES:SUBLANES + ts, :] = qkvz_ref[:, cols]
            w = convw_ref[:, cols]
            acc = buf[SUBLANES:SUBLANES + ts, :] * w[CONV_K - 1:CONV_K]
            for j in range(CONV_K - 1):
                off = SUBLANES - (CONV_K - 1) + j
                acc = acc + buf[off:off + ts, :] * w[j:j + 1]
            y = _silu(acc)
            if sec != SEC_V:
                scale = DN_DK ** -0.5 if sec == SEC_Q else 1.0
                y = y * (lax.rsqrt(jnp.sum(y * y, -1, keepdims=True) + RMS_EPS) * scale)
            dst[:, h * LANES:(h + 1) * LANES] = y

    sm = small_ref[...]
    beta_all = jax.nn.sigmoid(sm)
    g_all = -jnp.exp(apar_ref[0:1, :]) * jax.nn.softplus(sm + apar_ref[1:2, :])
    tr = lax.broadcasted_iota(jnp.int32, (ts, ts), 0)
    tc = lax.broadcasted_iota(jnp.int32, (ts, ts), 1)
    tril_bd = ((tr // CHUNK == tc // CHUNK) & (tr >= tc)).astype(F32)
    gcum_all = jnp.dot(tril_bd, g_all, preferred_element_type=F32,
                       precision=lax.Precision.HIGHEST)
    for h in range(heads):
        gb_s[:, h * LANES:(h + 1) * LANES] = jnp.broadcast_to(
            gcum_all[:, heads + h:heads + h + 1], (ts, LANES))
        bb_s[:, h * LANES:(h + 1) * LANES] = jnp.broadcast_to(beta_all[:, h:h + 1], (ts, LANES))

    onw = onw_ref[...]

    hr = range(heads)
    hs = [slice(h * LANES, (h + 1) * LANES) for h in hr]

    pairs = range(heads // 2)
    prow = lax.broadcasted_iota(jnp.int32, (CHUNK, LANES), 0)
    plane = lax.broadcasted_iota(jnp.int32, (CHUNK, LANES), 1)
    pcol = jnp.where(plane < CHUNK, plane, plane - CHUNK)
    left = plane < CHUNK
    tril_p = prow >= pcol
    strict_p = prow > pcol
    eye_b = prow == pcol
    eye_p = eye_b.astype(F32)
    first_head = lax.broadcasted_iota(jnp.int32, (CHUNK, 2 * LANES), 1) < LANES

    def block_diag(p):
        return jnp.concatenate([jnp.where(left, p, 0.0), jnp.where(left, 0.0, p)], axis=0)

    nchunk = ts // CHUNK

    def prep_chunks():
        units = [(cc, p) for cc in range(nchunk) for p in pairs]
        cidx = list(range(nchunk))
        rows = [pl.ds(c * CHUNK, CHUNK) for c in cidx]
        last = [pl.ds((c + 1) * CHUNK - 1, 1) for c in cidx]

        a, dec, kb = {}, {}, {}
        for (cc, p) in units:
            h1, h2 = 2 * p, 2 * p + 1
            cols = slice(h1 * LANES, (h2 + 1) * LANES)
            gi = jnp.where(left, gb_s[rows[cc], hs[h1]], gb_s[rows[cc], hs[h2]])
            gj = jnp.sum(jnp.where(eye_b, gi, 0.0), axis=0, keepdims=True)
            dec[cc, p] = jnp.where(tril_p, jnp.exp(jnp.minimum(gi - gj, 0.0)), 0.0)
            k2 = k_s[rows[cc], cols]
            kb[cc, p] = k2 * bb_s[rows[cc], cols]
            lhs = jnp.concatenate([kb[cc, p], q_s[rows[cc], cols]], axis=0)
            rhs = jnp.concatenate([jnp.where(first_head, k2, 0.0),
                                   jnp.where(first_head, 0.0, k2)], axis=0)
            a[cc, p] = _dot_nt(lhs, rhs)
        pw, tinv = {}, {}
        for u in units:
            cc, p = u
            attn = a[u][CHUNK:] * dec[u]
            at_s[cidx[cc], 2 * p] = attn[:, :CHUNK].astype(BF16)
            at_s[cidx[cc], 2 * p + 1] = attn[:, CHUNK:].astype(BF16)
            pw[u] = -jnp.where(strict_p, a[u][:CHUNK] * dec[u], 0.0)
            tinv[u] = eye_p + pw[u]
        for u in units:
            pw[u] = _dot(pw[u], block_diag(pw[u]))
        for _ in range(4):
            r = {u: _dot(jnp.concatenate([tinv[u], pw[u]], axis=0), block_diag(pw[u])) for u in units}
            for u in units:
                tinv[u] = tinv[u] + r[u][:CHUNK]
                pw[u] = r[u][CHUNK:]
        for u in units:
            tinv[u] = tinv[u] + _dot(tinv[u], block_diag(pw[u]))
        uw = {}
        for (cc, p) in units:
            for j, h in enumerate((2 * p, 2 * p + 1)):
                th = tinv[cc, p][:, j * CHUNK:(j + 1) * CHUNK]
                eg = jnp.exp(gb_s[rows[cc], hs[h]])
                kbh = kb[cc, p][:, j * LANES:(j + 1) * LANES]
                uw[cc, h] = _dot(th, jnp.concatenate(
                    [v_s[rows[cc], hs[h]] * bb_s[rows[cc], hs[h]], kbh * eg], axis=1))
        for cc in range(nchunk):
            c = cidx[cc]
            for h in hr:
                gi = gb_s[rows[cc], hs[h]]
                glast = gb_s[last[cc], hs[h]]
                u_s[c, h] = uw[cc, h][:, :DN_DV]
                wq_s[c, h] = jnp.concatenate([uw[cc, h][:, DN_DV:], q_s[rows[cc], hs[h]] * jnp.exp(gi)],
                                             axis=0).astype(BF16)
                kd_s[c, h] = (k_s[rows[cc], hs[h]] * jnp.exp(glast - gi)).astype(BF16)
                el_s[c, h:h + 1, :] = jnp.exp(glast)

    def scan_chunk(c):
        rows = pl.ds(c * CHUNK, CHUNK)
        s_old = [s_ref[h] for h in hr]
        sw = [_dot(wq_s[c, h], s_old[h]) for h in hr]
        v_new = [u_s[c, h] - sw[h][:CHUNK] for h in hr]
        o = [sw[h][CHUNK:] + _dot(at_s[c, h], v_new[h]) for h in hr]
        for h in hr:
            s_ref[h] = s_old[h] * el_s[c, h:h + 1, :] + _dot_tn(kd_s[c, h], v_new[h])
        for h in hr:
            z = qkvz_ref[rows, SEC_Z * d + h * LANES:SEC_Z * d + (h + 1) * LANES]
            y = (o[h] * lax.rsqrt(jnp.mean(o[h] * o[h], -1, keepdims=True) + RMS_EPS)
                 * onw * _silu(z))
            y_ref[rows, hs[h]] = y.astype(y_ref.dtype)

    prep_chunks()
    for c in range(nchunk):
        scan_chunk(c)


def _deltanet(proj, convw, apar, onw, batch, seq, d, ts):
    t_total = proj.shape[0]
    nt = seq // ts
    small_blk = 8 * d // LANES
    kern = functools.partial(_deltanet_kernel, ts=ts, d=d)
    return pl.pallas_call(
        kern,
        grid=(batch, nt),
        in_specs=[
            pl.BlockSpec((ts, 4 * d), lambda b, t: (b * nt + t, 0)),
            pl.BlockSpec((SUBLANES, 4 * d),
                         lambda b, t: (jnp.maximum((b * nt + t) * (ts // SUBLANES) - 1, 0), 0)),
            pl.BlockSpec((ts, LANES), lambda b, t: (b * nt + t, small_blk)),
            pl.BlockSpec((CONV_K, 3 * d), lambda b, t: (0, 0)),
            pl.BlockSpec((2, LANES), lambda b, t: (0, 0)),
            pl.BlockSpec((1, LANES), lambda b, t: (0, 0)),
        ],
        out_specs=pl.BlockSpec((ts, d), lambda b, t: (b * nt + t, 0)),
        out_shape=jax.ShapeDtypeStruct((t_total, d), BF16),
        scratch_shapes=[
            pltpu.VMEM((DN_HEADS, DN_DK, DN_DV), F32),
            pltpu.VMEM((ts, d), F32),
            pltpu.VMEM((ts, d), F32),
            pltpu.VMEM((ts, d), F32),
            pltpu.VMEM((ts, d), F32),
            pltpu.VMEM((ts, d), F32),
            pltpu.VMEM((ts // CHUNK, DN_HEADS, CHUNK, DN_DV), F32),
            pltpu.VMEM((ts // CHUNK, DN_HEADS, 2 * CHUNK, DN_DK), BF16),
            pltpu.VMEM((ts // CHUNK, DN_HEADS, CHUNK, CHUNK), BF16),
            pltpu.VMEM((ts // CHUNK, DN_HEADS, CHUNK, DN_DK), BF16),
            pltpu.VMEM((ts // CHUNK, DN_HEADS, LANES), F32),
            pltpu.VMEM((CONV_BUFS, ts + SUBLANES, LANES), F32),
        ],
        compiler_params=pltpu.CompilerParams(
            dimension_semantics=("arbitrary", "arbitrary"),
            vmem_limit_bytes=VMEM_LIMIT),
        name="deltanet",
    )(proj, proj, proj, convw, apar, onw)


def _sgu_kernel(u_ref, vg_ref, ws_ref, bst_ref, lng_ref, lnb_ref, y_ref, *, nblk):
    ri = lax.broadcasted_iota(jnp.int32, (SGU_BLOCK, SGU_BLOCK), 0) // CHUNK
    ci = lax.broadcasted_iota(jnp.int32, (SGU_BLOCK, SGU_BLOCK), 1) // CHUNK
    causal = ri >= ci
    lng = lng_ref[...]
    lnb = lnb_ref[...]
    for n in range(nblk):
        rows = slice(n * SGU_BLOCK, (n + 1) * SGU_BLOCK)
        vn = _layer_norm(_gelu(vg_ref[rows, :]), lng, lnb)
        for g in range(SGU_GROUPS):
            cs = slice(g * SGU_GROUP_DIM, (g + 1) * SGU_GROUP_DIM)
            ws = jnp.where(causal, ws_ref[g], 0.0)
            sp = _dot(ws, vn[:, cs]) + bst_ref[:, g:g + 1]
            y_ref[rows, cs] = (_gelu(u_ref[rows, cs]) * sp).astype(y_ref.dtype)


def _sgu(proj, w_s, b_s_t, ln_g, ln_b, d, nblk):
    t_total = proj.shape[0]
    tm = nblk * SGU_BLOCK
    kern = functools.partial(_sgu_kernel, nblk=nblk)
    return pl.pallas_call(
        kern,
        grid=(t_total // tm,),
        in_specs=[
            pl.BlockSpec((tm, d), lambda i: (i, SEC_U)),
            pl.BlockSpec((tm, d), lambda i: (i, SEC_VG)),
            pl.BlockSpec((SGU_GROUPS, SGU_BLOCK, SGU_BLOCK), lambda i: (0, 0, 0)),
            pl.BlockSpec((SGU_BLOCK, SGU_GROUPS), lambda i: (0, 0)),
            pl.BlockSpec((1, d), lambda i: (0, 0)),
            pl.BlockSpec((1, d), lambda i: (0, 0)),
        ],
        out_specs=pl.BlockSpec((tm, d), lambda i: (i, 0)),
        out_shape=jax.ShapeDtypeStruct((t_total, d), BF16),
        compiler_params=pltpu.CompilerParams(
            dimension_semantics=("arbitrary",), vmem_limit_bytes=VMEM_LIMIT),
        name="sgu",
    )(proj, proj, w_s, b_s_t, ln_g, ln_b)


def _merge_kernel(ya_ref, yb_ref, ga_ref, gb_ref, x_ref, wpa_ref, wpb_ref, wo_ref,
                  g_ref, b_ref, o_ref, ob_ref, *, alpha):
    pa = jnp.dot(ya_ref[...], wpa_ref[...], preferred_element_type=F32)
    pb = jnp.dot(yb_ref[...], wpb_ref[...], preferred_element_type=F32)
    m = jax.nn.sigmoid(ga_ref[...]) * pa + jax.nn.sigmoid(gb_ref[...]) * pb
    mix = jnp.dot(m.astype(BF16), wo_ref[...], preferred_element_type=F32)
    out = _layer_norm(alpha * x_ref[...] + mix, g_ref[...], b_ref[...])
    o_ref[...] = out
    ob_ref[...] = out.astype(BF16)


def _resident(shape):
    return pl.BlockSpec(shape, lambda i: (0,) * len(shape), pipeline_mode=pl.Buffered(1))


def _merge(ya, yb, proj, x, wpa, wpb, wo, ln_g, ln_b, alpha, tm):
    t_total, d = x.shape
    kern = functools.partial(_merge_kernel, alpha=alpha)
    row = pl.BlockSpec((tm, d), lambda i: (i, 0))
    return pl.pallas_call(
        kern,
        grid=(t_total // tm,),
        in_specs=[row, row,
                  pl.BlockSpec((tm, d), lambda i: (i, SEC_GA)),
                  pl.BlockSpec((tm, d), lambda i: (i, SEC_GB)),
                  row,
                  _resident((d, d)), _resident((d, d)), _resident((d, d)),
                  _resident((1, d)), _resident((1, d))],
        out_specs=[row, row],
        out_shape=[jax.ShapeDtypeStruct((t_total, d), F32),
                   jax.ShapeDtypeStruct((t_total, d), BF16)],
        compiler_params=pltpu.CompilerParams(
            dimension_semantics=("arbitrary",), vmem_limit_bytes=VMEM_LIMIT),
        name="merge",
    )(ya, yb, proj, proj, x, wpa, wpb, wo, ln_g, ln_b)


def _ffn_kernel(x_ref, xb_ref, wg_ref, wu_ref, wd_ref, g_ref, b_ref, o_ref, ob_ref, *, alpha):
    xb = xb_ref[...]
    gate = jnp.dot(xb, wg_ref[...], preferred_element_type=F32)
    up = jnp.dot(xb, wu_ref[...], preferred_element_type=F32)
    hid = (_silu(gate) * up).astype(BF16)
    ffn = jnp.dot(hid, wd_ref[...], preferred_element_type=F32)
    out = _layer_norm(alpha * x_ref[...] + ffn, g_ref[...], b_ref[...])
    o_ref[...] = out
    ob_ref[...] = out.astype(BF16)


def _ffn(x, xb, wg, wu, wd, ln_g, ln_b, alpha, tm):
    t_total, d = x.shape
    hdim = wg.shape[1]
    kern = functools.partial(_ffn_kernel, alpha=alpha)
    row = pl.BlockSpec((tm, d), lambda i: (i, 0))
    return pl.pallas_call(
        kern,
        grid=(t_total // tm,),
        in_specs=[row, row,
                  _resident((d, hdim)), _resident((d, hdim)), _resident((hdim, d)),
                  _resident((1, d)), _resident((1, d))],
        out_specs=[row, row],
        out_shape=[jax.ShapeDtypeStruct((t_total, d), F32),
                   jax.ShapeDtypeStruct((t_total, d), BF16)],
        compiler_params=pltpu.CompilerParams(
            dimension_semantics=("arbitrary",), vmem_limit_bytes=VMEM_LIMIT),
        name="ffn",
    )(x, xb, wg, wu, wd, ln_g, ln_b)


def _pick_tile(n, want):
    tile = min(n, want)
    assert n % tile == 0, (n, tile)
    return tile


def kernel(x, w_in, conv_w, a_log, dt_bias, o_norm_w, sgu_ln_g, sgu_ln_b, w_s, b_s, w_pa, w_pb, w_o, ln1_g, ln1_b, w_ffn_gate, w_ffn_up, w_ffn_down, ln2_g, ln2_b):
    batch, seq, d = x.shape
    depth = w_in.shape[0]
    assert d == DN_HEADS * DN_DK == SGU_GROUPS * SGU_GROUP_DIM
    assert seq % SGU_BLOCK == 0
    alpha = (2 * depth) ** 0.25
    t_total = batch * seq

    xf = x.reshape(t_total, d)
    xb = xf.astype(BF16)
    pad = jnp.zeros((d, LANES - 2 * DN_HEADS), F32)
    lane_pad = jnp.zeros((DN_HEADS,), F32)
    for l in range(depth):
        wl = w_in[l]
        w_main = jnp.concatenate([wl[:, :4 * d], wl[:, 4 * d + 2 * DN_HEADS:]], axis=1)
        w_small = jnp.concatenate([wl[:, 4 * d:4 * d + 2 * DN_HEADS], pad], axis=1)
        w_all = jnp.concatenate([w_main, w_small], axis=1).astype(BF16)
        n_all = w_all.shape[1]
        proj = _in_proj(xb, w_all, _pick_tile(t_total, 1024), n_all // 5)

        apar = jnp.stack([
            jnp.concatenate([lane_pad, a_log[l], jnp.zeros((LANES - 2 * DN_HEADS,), F32)]),
            jnp.concatenate([lane_pad, dt_bias[l], jnp.zeros((LANES - 2 * DN_HEADS,), F32)]),
        ])
        ya = _deltanet(proj, conv_w[l], apar, o_norm_w[l].reshape(1, DN_DV),
                       batch, seq, d, _pick_tile(seq, 256))
        yb = _sgu(proj, w_s[l], b_s[l].T, sgu_ln_g[l].reshape(1, d), sgu_ln_b[l].reshape(1, d),
                  d, _pick_tile(t_total // SGU_BLOCK, 4))
        xf, xb = _merge(ya, yb, proj, xf, w_pa[l].astype(BF16), w_pb[l].astype(BF16),
                        w_o[l].astype(BF16), ln1_g[l].reshape(1, d), ln1_b[l].reshape(1, d),
                        alpha, _pick_tile(t_total, 512))
        xf, xb = _ffn(xf, xb, w_ffn_gate[l].astype(BF16), w_ffn_up[l].astype(BF16),
                      w_ffn_down[l].astype(BF16), ln2_g[l].reshape(1, d), ln2_b[l].reshape(1, d),
                      alpha, _pick_tile(t_total, 256))
    return xf.reshape(batch, seq, d)
```

```python
import functools
import math

import jax
import jax.numpy as jnp
from jax import lax
from jax.experimental import pallas as pl
from jax.experimental.pallas import tpu as pltpu

F32 = jnp.float32
BF16 = jnp.bfloat16

CHUNK = 64
DN_HEADS = 8
DN_DK = 128
DN_DV = 128
CONV_K = 4
SGU_BLOCK = 128
SGU_GROUPS = 8
SGU_GROUP_DIM = 128
LN_EPS = 1e-5
RMS_EPS = 1e-6
LOG2E = math.log2(math.e)
CONV_HIST = 16
CONV_BUFS = 4

LANES = 128
SUBLANES = 8
MXU_COLS = 256
REST_COLS = 512
VMEM_LIMIT = 56 * 1024 * 1024

SEC_Q, SEC_K, SEC_V = range(3)
SEC_Z, SEC_U, SEC_VG, SEC_GA, SEC_GB = range(5)


def _layer_norm(h, g, b):
    mu = jnp.mean(h, -1, keepdims=True)
    d = h - mu
    var = jnp.mean(d * d, -1, keepdims=True)
    return d * lax.rsqrt(var + LN_EPS) * g + b


def _gelu(x):
    return 0.5 * x * (1.0 + lax.erf(x * (1.0 / math.sqrt(2.0))))


def _silu(x):
    return x / (1.0 + jnp.exp2(x * (-LOG2E)))


def _dot(a, b):
    return jnp.dot(a.astype(BF16), b.astype(BF16), preferred_element_type=F32)


def _dot_nt(a, b):
    return lax.dot_general(a.astype(BF16), b.astype(BF16), (((1,), (1,)), ((), ())),
                           preferred_element_type=F32)


def _dot_tn(a, b):
    return lax.dot_general(a.astype(BF16), b.astype(BF16), (((0,), (0,)), ((), ())),
                           preferred_element_type=F32)


def _resident(shape):
    return pl.BlockSpec(shape, lambda i: (0,) * len(shape), pipeline_mode=pl.Buffered(1))


def _in_proj_kernel(x_ref, xh_ref, wqkv_ref, wrest_ref, cw_ref, qkv_ref, rest_ref, stage,
                    *, tm, d, tiles_per_seq):
    i = pl.program_id(0)
    keep_hist = (i % tiles_per_seq != 0).astype(F32)
    xb = x_ref[...].astype(BF16)
    xx = jnp.concatenate([xh_ref[...].astype(BF16), xb], axis=0)
    nrest = wrest_ref.shape[1]
    assert nrest % REST_COLS == 0 and nrest // REST_COLS <= 3 * d // MXU_COLS
    for c0 in range(0, 3 * d, MXU_COLS):
        sec = c0 // d
        p = jnp.dot(xx, wqkv_ref[:, c0:c0 + MXU_COLS], preferred_element_type=F32)
        for s0 in range(0, MXU_COLS, LANES):
            cols = slice(c0 + s0, c0 + s0 + LANES)
            buf = stage.at[(c0 + s0) // LANES % CONV_BUFS]
            buf[0:CONV_HIST, :] = p[0:CONV_HIST, s0:s0 + LANES] * keep_hist
            buf[CONV_HIST:CONV_HIST + tm, :] = p[CONV_HIST:, s0:s0 + LANES]
            w = cw_ref[:, cols]
            acc = buf[CONV_HIST:CONV_HIST + tm, :] * w[CONV_K - 1:CONV_K]
            for j in range(CONV_K - 1):
                off = CONV_HIST - (CONV_K - 1) + j
                acc = acc + buf[off:off + tm, :] * w[j:j + 1]
            y = _silu(acc)
            if sec != SEC_V:
                scale = DN_DK ** -0.5 if sec == SEC_Q else 1.0
                y = y * (lax.rsqrt(jnp.sum(y * y, -1, keepdims=True) + RMS_EPS) * scale)
            qkv_ref[:, cols] = y
        g = c0 // MXU_COLS
        if g < nrest // REST_COLS:
            rc = slice(g * REST_COLS, (g + 1) * REST_COLS)
            rest_ref[:, rc] = jnp.dot(xb, wrest_ref[:, rc],
                                      preferred_element_type=F32).astype(rest_ref.dtype)


def _in_proj(x, w_qkv, w_rest, convw, seq, tm):
    t, d = x.shape
    nrest = w_rest.shape[1]
    hist_blocks = tm // CONV_HIST
    kern = functools.partial(_in_proj_kernel, tm=tm, d=d, tiles_per_seq=seq // tm)
    return pl.pallas_call(
        kern,
        grid=(t // tm,),
        in_specs=[pl.BlockSpec((tm, d), lambda i: (i, 0)),
                  pl.BlockSpec((CONV_HIST, d), lambda i: (jnp.maximum(i * hist_blocks - 1, 0), 0)),
                  _resident((d, 3 * d)), _resident((d, nrest)), _resident((CONV_K, 3 * d))],
        out_specs=[pl.BlockSpec((tm, 3 * d), lambda i: (i, 0)),
                   pl.BlockSpec((tm, nrest), lambda i: (i, 0))],
        out_shape=[jax.ShapeDtypeStruct((t, 3 * d), F32),
                   jax.ShapeDtypeStruct((t, nrest), BF16)],
        scratch_shapes=[pltpu.VMEM((CONV_BUFS, CONV_HIST + tm, LANES), F32)],
        compiler_params=pltpu.CompilerParams(
            dimension_semantics=("arbitrary",), vmem_limit_bytes=VMEM_LIMIT),
        name="in_proj",
    )(x, x, w_qkv, w_rest, convw)


def _deltanet_kernel(qkv_ref, z_ref, x_ref, wsm_ref, apar_ref, onw_ref,
                     y_ref, s_ref, gb_s, bb_s, u_s, wq_s, at_s, kd_s, el_s, *, ts, d):
    t = pl.program_id(1)
    heads = DN_HEADS

    @pl.when(t == 0)
    def _():
        s_ref[...] = jnp.zeros_like(s_ref)

    def q_at(rows, cols):
        return qkv_ref[rows, SEC_Q * d + cols.start:SEC_Q * d + cols.stop]

    def k_at(rows, cols):
        return qkv_ref[rows, SEC_K * d + cols.start:SEC_K * d + cols.stop]

    def v_at(rows, cols):
        return qkv_ref[rows, SEC_V * d + cols.start:SEC_V * d + cols.stop]

    sm = jnp.dot(x_ref[...].astype(BF16), wsm_ref[...], preferred_element_type=F32)
    beta_all = jax.nn.sigmoid(sm)
    g_all = -jnp.exp(apar_ref[0:1, :]) * jax.nn.softplus(sm + apar_ref[1:2, :])
    tr = lax.broadcasted_iota(jnp.int32, (ts, ts), 0)
    tc = lax.broadcasted_iota(jnp.int32, (ts, ts), 1)
    tril_bd = ((tr // CHUNK == tc // CHUNK) & (tr >= tc)).astype(F32)
    gcum_all = jnp.dot(tril_bd, g_all, preferred_element_type=F32,
                       precision=lax.Precision.HIGHEST)
    for h in range(heads):
        gb_s[:, h * LANES:(h + 1) * LANES] = jnp.broadcast_to(
            gcum_all[:, heads + h:heads + h + 1], (ts, LANES))
        bb_s[:, h * LANES:(h + 1) * LANES] = jnp.broadcast_to(beta_all[:, h:h + 1], (ts, LANES))

    onw = onw_ref[...]

    hr = range(heads)
    hs = [slice(h * LANES, (h + 1) * LANES) for h in hr]
    pairs = range(heads // 2)
    prow = lax.broadcasted_iota(jnp.int32, (CHUNK, LANES), 0)
    plane = lax.broadcasted_iota(jnp.int32, (CHUNK, LANES), 1)
    pcol = jnp.where(plane < CHUNK, plane, plane - CHUNK)
    left = plane < CHUNK
    tril_p = prow >= pcol
    strict_p = prow > pcol
    eye_b = prow == pcol
    eye_p = eye_b.astype(F32)
    first_head = lax.broadcasted_iota(jnp.int32, (CHUNK, 2 * LANES), 1) < LANES

    def block_diag(p):
        return jnp.concatenate([jnp.where(left, p, 0.0), jnp.where(left, 0.0, p)], axis=0)

    nchunk = ts // CHUNK

    def prep_chunks():
        units = [(cc, p) for cc in range(nchunk) for p in pairs]
        cidx = list(range(nchunk))
        rows = [pl.ds(c * CHUNK, CHUNK) for c in cidx]
        last = [pl.ds((c + 1) * CHUNK - 1, 1) for c in cidx]

        a, dec, kb = {}, {}, {}
        for (cc, p) in units:
            h1, h2 = 2 * p, 2 * p + 1
            cols = slice(h1 * LANES, (h2 + 1) * LANES)
            gi = jnp.where(left, gb_s[rows[cc], hs[h1]], gb_s[rows[cc], hs[h2]])
            gj = jnp.sum(jnp.where(eye_b, gi, 0.0), axis=0, keepdims=True)
            dec[cc, p] = jnp.where(tril_p, jnp.exp(jnp.minimum(gi - gj, 0.0)), 0.0)
            k2 = k_at(rows[cc], cols)
            kb[cc, p] = k2 * bb_s[rows[cc], cols]
            lhs = jnp.concatenate([kb[cc, p], q_at(rows[cc], cols)], axis=0)
            rhs = jnp.concatenate([jnp.where(first_head, k2, 0.0),
                                   jnp.where(first_head, 0.0, k2)], axis=0)
            a[cc, p] = _dot_nt(lhs, rhs)
        pw, tinv = {}, {}
        for u in units:
            cc, p = u
            attn = a[u][CHUNK:] * dec[u]
            at_s[cidx[cc], 2 * p] = attn[:, :CHUNK].astype(BF16)
            at_s[cidx[cc], 2 * p + 1] = attn[:, CHUNK:].astype(BF16)
            pw[u] = -jnp.where(strict_p, a[u][:CHUNK] * dec[u], 0.0)
            tinv[u] = eye_p + pw[u]
        for u in units:
            pw[u] = _dot(pw[u], block_diag(pw[u]))
        for _ in range(4):
            r = {u: _dot(jnp.concatenate([tinv[u], pw[u]], axis=0), block_diag(pw[u])) for u in units}
            for u in units:
                tinv[u] = tinv[u] + r[u][:CHUNK]
                pw[u] = r[u][CHUNK:]
        for u in units:
            tinv[u] = tinv[u] + _dot(tinv[u], block_diag(pw[u]))
        uw = {}
        for (cc, p) in units:
            for j, h in enumerate((2 * p, 2 * p + 1)):
                th = tinv[cc, p][:, j * CHUNK:(j + 1) * CHUNK]
                eg = jnp.exp(gb_s[rows[cc], hs[h]])
                kbh = kb[cc, p][:, j * LANES:(j + 1) * LANES]
                uw[cc, h] = _dot(th, jnp.concatenate(
                    [v_at(rows[cc], hs[h]) * bb_s[rows[cc], hs[h]], kbh * eg], axis=1))
        for cc in range(nchunk):
            c = cidx[cc]
            for h in hr:
                gi = gb_s[rows[cc], hs[h]]
                glast = gb_s[last[cc], hs[h]]
                u_s[c, h] = uw[cc, h][:, :DN_DV]
                wq_s[c, h] = jnp.concatenate([uw[cc, h][:, DN_DV:], q_at(rows[cc], hs[h]) * jnp.exp(gi)],
                                             axis=0).astype(BF16)
                kd_s[c, h] = (k_at(rows[cc], hs[h]) * jnp.exp(glast - gi)).astype(BF16)
                el_s[c, h:h + 1, :] = jnp.exp(glast)

    def scan_chunk(c):
        rows = pl.ds(c * CHUNK, CHUNK)
        s_old = [s_ref[h] for h in hr]
        sw = [_dot(wq_s[c, h], s_old[h]) for h in hr]
        v_new = [u_s[c, h] - sw[h][:CHUNK] for h in hr]
        o = [sw[h][CHUNK:] + _dot(at_s[c, h], v_new[h]) for h in hr]
        for h in hr:
            s_ref[h] = s_old[h] * el_s[c, h:h + 1, :] + _dot_tn(kd_s[c, h], v_new[h])
        for h in hr:
            z = z_ref[rows, hs[h]].astype(F32)
            y = (o[h] * lax.rsqrt(jnp.mean(o[h] * o[h], -1, keepdims=True) + RMS_EPS)
                 * onw * _silu(z))
            y_ref[rows, hs[h]] = y.astype(y_ref.dtype)

    prep_chunks()
    for c in range(nchunk):
        scan_chunk(c)


def _deltanet(qkv, rest, x, w_small, apar, onw, batch, seq, d, ts):
    t_total = x.shape[0]
    nt = seq // ts
    kern = functools.partial(_deltanet_kernel, ts=ts, d=d)
    return pl.pallas_call(
        kern,
        grid=(batch, nt),
        in_specs=[
            pl.BlockSpec((ts, 3 * d), lambda b, t: (b * nt + t, 0)),
            pl.BlockSpec((ts, d), lambda b, t: (b * nt + t, SEC_Z)),
            pl.BlockSpec((ts, d), lambda b, t: (b * nt + t, 0)),
            pl.BlockSpec((d, LANES), lambda b, t: (0, 0)),
            pl.BlockSpec((2, LANES), lambda b, t: (0, 0)),
            pl.BlockSpec((1, LANES), lambda b, t: (0, 0)),
        ],
        out_specs=pl.BlockSpec((ts, d), lambda b, t: (b * nt + t, 0)),
        out_shape=jax.ShapeDtypeStruct((t_total, d), BF16),
        scratch_shapes=[
            pltpu.VMEM((DN_HEADS, DN_DK, DN_DV), F32),
            pltpu.VMEM((ts, d), F32),
            pltpu.VMEM((ts, d), F32),
            pltpu.VMEM((ts // CHUNK, DN_HEADS, CHUNK, DN_DV), F32),
            pltpu.VMEM((ts // CHUNK, DN_HEADS, 2 * CHUNK, DN_DK), BF16),
            pltpu.VMEM((ts // CHUNK, DN_HEADS, CHUNK, CHUNK), BF16),
            pltpu.VMEM((ts // CHUNK, DN_HEADS, CHUNK, DN_DK), BF16),
            pltpu.VMEM((ts // CHUNK, DN_HEADS, LANES), F32),
        ],
        compiler_params=pltpu.CompilerParams(
            dimension_semantics=("arbitrary", "arbitrary"),
            vmem_limit_bytes=VMEM_LIMIT),
        name="deltanet",
    )(qkv, rest, x, w_small, apar, onw)


def _sgu_kernel(u_ref, vg_ref, ws_ref, bst_ref, lng_ref, lnb_ref, y_ref, *, nblk):
    ri = lax.broadcasted_iota(jnp.int32, (SGU_BLOCK, SGU_BLOCK), 0) // CHUNK
    ci = lax.broadcasted_iota(jnp.int32, (SGU_BLOCK, SGU_BLOCK), 1) // CHUNK
    causal = ri >= ci
    lng = lng_ref[...]
    lnb = lnb_ref[...]
    for n in range(nblk):
        rows = slice(n * SGU_BLOCK, (n + 1) * SGU_BLOCK)
        vn = _layer_norm(_gelu(vg_ref[rows, :].astype(F32)), lng, lnb)
        for g in range(SGU_GROUPS):
            cs = slice(g * SGU_GROUP_DIM, (g + 1) * SGU_GROUP_DIM)
            ws = jnp.where(causal, ws_ref[g], 0.0)
            sp = _dot(ws, vn[:, cs]) + bst_ref[:, g:g + 1]
            y_ref[rows, cs] = (_gelu(u_ref[rows, cs].astype(F32)) * sp).astype(y_ref.dtype)


def _sgu(proj, w_s, b_s_t, ln_g, ln_b, d, nblk):
    t_total = proj.shape[0]
    tm = nblk * SGU_BLOCK
    kern = functools.partial(_sgu_kernel, nblk=nblk)
    return pl.pallas_call(
        kern,
        grid=(t_total // tm,),
        in_specs=[
            pl.BlockSpec((tm, d), lambda i: (i, SEC_U)),
            pl.BlockSpec((tm, d), lambda i: (i, SEC_VG)),
            pl.BlockSpec((SGU_GROUPS, SGU_BLOCK, SGU_BLOCK), lambda i: (0, 0, 0)),
            pl.BlockSpec((SGU_BLOCK, SGU_GROUPS), lambda i: (0, 0)),
            pl.BlockSpec((1, d), lambda i: (0, 0)),
            pl.BlockSpec((1, d), lambda i: (0, 0)),
        ],
        out_specs=pl.BlockSpec((tm, d), lambda i: (i, 0)),
        out_shape=jax.ShapeDtypeStruct((t_total, d), BF16),
        compiler_params=pltpu.CompilerParams(
            dimension_semantics=("arbitrary",), vmem_limit_bytes=VMEM_LIMIT),
        name="sgu",
    )(proj, proj, w_s, b_s_t, ln_g, ln_b)


def _merge_kernel(ya_ref, yb_ref, ga_ref, gb_ref, x_ref, wpa_ref, wpb_ref, wo_ref,
                  g_ref, b_ref, o_ref, *, alpha):
    pa = jnp.dot(ya_ref[...], wpa_ref[...], preferred_element_type=F32)
    pb = jnp.dot(yb_ref[...], wpb_ref[...], preferred_element_type=F32)
    m = (jax.nn.sigmoid(ga_ref[...].astype(F32)) * pa
         + jax.nn.sigmoid(gb_ref[...].astype(F32)) * pb)
    mix = jnp.dot(m.astype(BF16), wo_ref[...], preferred_element_type=F32)
    o_ref[...] = _layer_norm(alpha * x_ref[...] + mix, g_ref[...], b_ref[...])


def _merge(ya, yb, proj, x, wpa, wpb, wo, ln_g, ln_b, alpha, tm):
    t_total, d = x.shape
    kern = functools.partial(_merge_kernel, alpha=alpha)
    row = pl.BlockSpec((tm, d), lambda i: (i, 0))
    return pl.pallas_call(
        kern,
        grid=(t_total // tm,),
        in_specs=[row, row,
                  pl.BlockSpec((tm, d), lambda i: (i, SEC_GA)),
                  pl.BlockSpec((tm, d), lambda i: (i, SEC_GB)),
                  row,
                  _resident((d, d)), _resident((d, d)), _resident((d, d)),
                  _resident((1, d)), _resident((1, d))],
        out_specs=row,
        out_shape=jax.ShapeDtypeStruct((t_total, d), F32),
        compiler_params=pltpu.CompilerParams(
            dimension_semantics=("arbitrary",), vmem_limit_bytes=VMEM_LIMIT),
        name="merge",
    )(ya, yb, proj, proj, x, wpa, wpb, wo, ln_g, ln_b)


def _ffn_kernel(x_ref, wg_ref, wu_ref, wd_ref, g_ref, b_ref, o_ref, *, alpha):
    xb = x_ref[...].astype(BF16)
    gate = jnp.dot(xb, wg_ref[...], preferred_element_type=F32)
    up = jnp.dot(xb, wu_ref[...], preferred_element_type=F32)
    hid = (_silu(gate) * up).astype(BF16)
    ffn = jnp.dot(hid, wd_ref[...], preferred_element_type=F32)
    o_ref[...] = _layer_norm(alpha * x_ref[...] + ffn, g_ref[...], b_ref[...])


def _ffn(x, wg, wu, wd, ln_g, ln_b, alpha, tm):
    t_total, d = x.shape
    hdim = wg.shape[1]
    kern = functools.partial(_ffn_kernel, alpha=alpha)
    row = pl.BlockSpec((tm, d), lambda i: (i, 0))
    return pl.pallas_call(
        kern,
        grid=(t_total // tm,),
        in_specs=[row,
                  _resident((d, hdim)), _resident((d, hdim)), _resident((hdim, d)),
                  _resident((1, d)), _resident((1, d))],
        out_specs=row,
        out_shape=jax.ShapeDtypeStruct((t_total, d), F32),
        compiler_params=pltpu.CompilerParams(
            dimension_semantics=("arbitrary",), vmem_limit_bytes=VMEM_LIMIT),
        name="ffn",
    )(x, wg, wu, wd, ln_g, ln_b)


def _pick_tile(n, want):
    tile = min(n, want)
    assert n % tile == 0, (n, tile)
    return tile


def kernel(x, w_in, conv_w, a_log, dt_bias, o_norm_w, sgu_ln_g, sgu_ln_b, w_s, b_s, w_pa, w_pb, w_o, ln1_g, ln1_b, w_ffn_gate, w_ffn_up, w_ffn_down, ln2_g, ln2_b):
    batch, seq, d = x.shape
    depth = w_in.shape[0]
    assert d == DN_HEADS * DN_DK == SGU_GROUPS * SGU_GROUP_DIM
    assert seq % SGU_BLOCK == 0
    alpha = (2 * depth) ** 0.25
    t_total = batch * seq

    xf = x.reshape(t_total, d)
    nsm = 2 * DN_HEADS
    lane_pad = jnp.zeros((DN_HEADS,), F32)
    tail_pad = jnp.zeros((LANES - nsm,), F32)
    for l in range(depth):
        wl = w_in[l]
        w_qkv = wl[:, :3 * d].astype(BF16)
        w_rest = jnp.concatenate([wl[:, 3 * d:4 * d].astype(BF16),
                                  wl[:, 4 * d + nsm:].astype(BF16)], axis=1)
        w_small = jnp.concatenate([wl[:, 4 * d:4 * d + nsm].astype(BF16),
                                   jnp.zeros((d, LANES - nsm), BF16)], axis=1)
        qkv, rest = _in_proj(xf, w_qkv, w_rest, conv_w[l], seq, _pick_tile(seq, 512))

        apar = jnp.stack([jnp.concatenate([lane_pad, a_log[l], tail_pad]),
                          jnp.concatenate([lane_pad, dt_bias[l], tail_pad])])
        ya = _deltanet(qkv, rest, xf, w_small, apar, o_norm_w[l].reshape(1, DN_DV),
                       batch, seq, d, _pick_tile(seq, 256))
        yb = _sgu(rest, w_s[l], b_s[l].T, sgu_ln_g[l].reshape(1, d), sgu_ln_b[l].reshape(1, d),
                  d, _pick_tile(t_total // SGU_BLOCK, 4))
        xf = _merge(ya, yb, rest, xf, w_pa[l].astype(BF16), w_pb[l].astype(BF16),
                    w_o[l].astype(BF16), ln1_g[l].reshape(1, d), ln1_b[l].reshape(1, d),
                    alpha, _pick_tile(t_total, 512))
        xf = _ffn(xf, w_ffn_gate[l].astype(BF16), w_ffn_up[l].astype(BF16),
                  w_ffn_down[l].astype(BF16), ln2_g[l].reshape(1, d), ln2_b[l].reshape(1, d),
                  alpha, _pick_tile(t_total, 256))
    return xf.reshape(batch, seq, d)
```

```python
import functools
import math

import jax
import jax.numpy as jnp
from jax import lax
from jax.experimental import pallas as pl
from jax.experimental.pallas import tpu as pltpu

F32 = jnp.float32
BF16 = jnp.bfloat16

CHUNK = 64
DN_HEADS = 8
DN_DK = 128
DN_DV = 128
CONV_K = 4
SGU_BLOCK = 128
SGU_GROUPS = 8
SGU_GROUP_DIM = 128
LN_EPS = 1e-5
RMS_EPS = 1e-6
LOG2E = math.log2(math.e)
CONV_HIST = 16
CONV_BUFS = 4

LANES = 128
SUBLANES = 8
MXU_COLS = 256
REST_COLS = 512
VMEM_LIMIT = 56 * 1024 * 1024

SEC_Q, SEC_K, SEC_V = range(3)
SEC_Z, SEC_U, SEC_VG, SEC_GA, SEC_GB = range(5)


def _layer_norm(h, g, b):
    mu = jnp.mean(h, -1, keepdims=True)
    d = h - mu
    var = jnp.mean(d * d, -1, keepdims=True)
    return d * lax.rsqrt(var + LN_EPS) * g + b


def _gelu(x):
    return 0.5 * x * (1.0 + lax.erf(x * (1.0 / math.sqrt(2.0))))


def _silu(x):
    return x / (1.0 + jnp.exp2(x * (-LOG2E)))


def _dot(a, b):
    return jnp.dot(a.astype(BF16), b.astype(BF16), preferred_element_type=F32)


def _dot_nt(a, b):
    return lax.dot_general(a.astype(BF16), b.astype(BF16), (((1,), (1,)), ((), ())),
                           preferred_element_type=F32)


def _dot_tn(a, b):
    return lax.dot_general(a.astype(BF16), b.astype(BF16), (((0,), (0,)), ((), ())),
                           preferred_element_type=F32)


def _resident(shape, layer=None):
    if layer is None:
        return pl.BlockSpec(shape, lambda i: (0,) * len(shape), pipeline_mode=pl.Buffered(1))
    return pl.BlockSpec((None,) + tuple(shape), lambda i: (layer,) + (0,) * len(shape),
                        pipeline_mode=pl.Buffered(1))


def _in_proj_kernel(x_ref, xh_ref, whead_ref, wtail_ref, cw_ref, qkv_ref, rest_ref, stage,
                    *, tm, d, tiles_per_seq):
    i = pl.program_id(0)
    keep_hist = (i % tiles_per_seq != 0).astype(F32)
    xb = x_ref[...].astype(BF16)
    xx = jnp.concatenate([xh_ref[...].astype(BF16), xb], axis=0)
    nrest = rest_ref.shape[1]
    assert d % REST_COLS == 0 and nrest // REST_COLS <= 3 * d // MXU_COLS
    for c0 in range(0, 3 * d, MXU_COLS):
        sec = c0 // d
        p = jnp.dot(xx, whead_ref[:, c0:c0 + MXU_COLS], preferred_element_type=F32)
        for s0 in range(0, MXU_COLS, LANES):
            cols = slice(c0 + s0, c0 + s0 + LANES)
            buf = stage.at[(c0 + s0) // LANES % CONV_BUFS]
            buf[0:CONV_HIST, :] = p[0:CONV_HIST, s0:s0 + LANES] * keep_hist
            buf[CONV_HIST:CONV_HIST + tm, :] = p[CONV_HIST:, s0:s0 + LANES]
            w = cw_ref[:, cols]
            acc = buf[CONV_HIST:CONV_HIST + tm, :] * w[CONV_K - 1:CONV_K]
            for j in range(CONV_K - 1):
                off = CONV_HIST - (CONV_K - 1) + j
                acc = acc + buf[off:off + tm, :] * w[j:j + 1]
            y = _silu(acc)
            if sec != SEC_V:
                scale = DN_DK ** -0.5 if sec == SEC_Q else 1.0
                y = y * (lax.rsqrt(jnp.sum(y * y, -1, keepdims=True) + RMS_EPS) * scale)
            qkv_ref[:, cols] = y
        g = c0 // MXU_COLS
        if g < nrest // REST_COLS:
            r0 = g * REST_COLS
            rsec = r0 // d
            if rsec == SEC_Z:
                w = whead_ref[:, 3 * d + r0:3 * d + r0 + REST_COLS]
            else:
                w = wtail_ref[:, r0 - d:r0 - d + REST_COLS]
            r = jnp.dot(xb, w, preferred_element_type=F32)
            if rsec in (SEC_U, SEC_VG):
                r = _gelu(r)
            rest_ref[:, r0:r0 + REST_COLS] = r.astype(rest_ref.dtype)


def _in_proj(x, w_head, w_tail, convw, layer, seq, tm):
    t, d = x.shape
    nrest = w_head.shape[2] - 3 * d + w_tail.shape[2]
    hist_blocks = tm // CONV_HIST
    kern = functools.partial(_in_proj_kernel, tm=tm, d=d, tiles_per_seq=seq // tm)
    return pl.pallas_call(
        kern,
        grid=(t // tm,),
        in_specs=[pl.BlockSpec((tm, d), lambda i: (i, 0)),
                  pl.BlockSpec((CONV_HIST, d), lambda i: (jnp.maximum(i * hist_blocks - 1, 0), 0)),
                  _resident(w_head.shape[1:], layer), _resident(w_tail.shape[1:], layer),
                  _resident(convw.shape[1:], layer)],
        out_specs=[pl.BlockSpec((tm, 3 * d), lambda i: (i, 0)),
                   pl.BlockSpec((tm, nrest), lambda i: (i, 0))],
        out_shape=[jax.ShapeDtypeStruct((t, 3 * d), F32),
                   jax.ShapeDtypeStruct((t, nrest), BF16)],
        scratch_shapes=[pltpu.VMEM((CONV_BUFS, CONV_HIST + tm, LANES), F32)],
        compiler_params=pltpu.CompilerParams(
            dimension_semantics=("arbitrary",), vmem_limit_bytes=VMEM_LIMIT),
        name="in_proj",
    )(x, x, w_head, w_tail, convw)


def _deltanet_kernel(qkv_ref, z_ref, x_ref, wsm_ref, apar_ref, onw_ref,
                     y_ref, s_ref, gb_s, bb_s, u_s, wq_s, at_s, kd_s, el_s, *, ts, d):
    t = pl.program_id(1)
    heads = DN_HEADS
    slot = t % 2
    prev = 1 - slot

    @pl.when(t == 0)
    def _():
        s_ref[...] = jnp.zeros_like(s_ref)
        u_s[1] = jnp.zeros_like(u_s[1])
        wq_s[1] = jnp.zeros_like(wq_s[1])
        at_s[1] = jnp.zeros_like(at_s[1])
        kd_s[1] = jnp.zeros_like(kd_s[1])
        el_s[1] = jnp.zeros_like(el_s[1])

    def q_at(rows, cols):
        return qkv_ref[rows, SEC_Q * d + cols.start:SEC_Q * d + cols.stop]

    def k_at(rows, cols):
        return qkv_ref[rows, SEC_K * d + cols.start:SEC_K * d + cols.stop]

    def v_at(rows, cols):
        return qkv_ref[rows, SEC_V * d + cols.start:SEC_V * d + cols.stop]

    sm = jnp.dot(x_ref[...].astype(BF16), wsm_ref[...], preferred_element_type=F32)
    beta_all = jax.nn.sigmoid(sm)
    g_all = -jnp.exp(apar_ref[0:1, :]) * jax.nn.softplus(sm + apar_ref[1:2, :])
    tr = lax.broadcasted_iota(jnp.int32, (ts, ts), 0)
    tc = lax.broadcasted_iota(jnp.int32, (ts, ts), 1)
    tril_bd = ((tr // CHUNK == tc // CHUNK) & (tr >= tc)).astype(F32)
    gcum_all = jnp.dot(tril_bd, g_all, preferred_element_type=F32,
                       precision=lax.Precision.HIGHEST)
    for h in range(heads):
        gb_s[:, h * LANES:(h + 1) * LANES] = jnp.broadcast_to(
            gcum_all[:, heads + h:heads + h + 1], (ts, LANES))
        bb_s[:, h * LANES:(h + 1) * LANES] = jnp.broadcast_to(beta_all[:, h:h + 1], (ts, LANES))

    onw = onw_ref[...]

    hr = range(heads)
    hs = [slice(h * LANES, (h + 1) * LANES) for h in hr]
    pairs = range(heads // 2)
    prow = lax.broadcasted_iota(jnp.int32, (CHUNK, LANES), 0)
    plane = lax.broadcasted_iota(jnp.int32, (CHUNK, LANES), 1)
    pcol = jnp.where(plane < CHUNK, plane, plane - CHUNK)
    left = plane < CHUNK
    tril_p = prow >= pcol
    strict_p = prow > pcol
    eye_b = prow == pcol
    eye_p = eye_b.astype(F32)
    first_head = lax.broadcasted_iota(jnp.int32, (CHUNK, 2 * LANES), 1) < LANES

    def block_diag(p):
        return jnp.concatenate([jnp.where(left, p, 0.0), jnp.where(left, 0.0, p)], axis=0)

    nchunk = ts // CHUNK

    def prep_chunks():
        units = [(cc, p) for cc in range(nchunk) for p in pairs]
        cidx = list(range(nchunk))
        rows = [pl.ds(c * CHUNK, CHUNK) for c in cidx]
        last = [pl.ds((c + 1) * CHUNK - 1, 1) for c in cidx]

        a, dec, kb = {}, {}, {}
        for (cc, p) in units:
            h1, h2 = 2 * p, 2 * p + 1
            cols = slice(h1 * LANES, (h2 + 1) * LANES)
            gi = jnp.where(left, gb_s[rows[cc], hs[h1]], gb_s[rows[cc], hs[h2]])
            gj = jnp.sum(jnp.where(eye_b, gi, 0.0), axis=0, keepdims=True)
            dec[cc, p] = jnp.where(tril_p, jnp.exp(jnp.minimum(gi - gj, 0.0)), 0.0)
            k2 = k_at(rows[cc], cols)
            kb[cc, p] = k2 * bb_s[rows[cc], cols]
            lhs = jnp.concatenate([kb[cc, p], q_at(rows[cc], cols)], axis=0)
            rhs = jnp.concatenate([jnp.where(first_head, k2, 0.0),
                                   jnp.where(first_head, 0.0, k2)], axis=0)
            a[cc, p] = _dot_nt(lhs, rhs)
        pw, tinv = {}, {}
        for u in units:
            cc, p = u
            attn = a[u][CHUNK:] * dec[u]
            at_s[slot, cidx[cc], 2 * p] = attn[:, :CHUNK].astype(BF16)
            at_s[slot, cidx[cc], 2 * p + 1] = attn[:, CHUNK:].astype(BF16)
            pw[u] = -jnp.where(strict_p, a[u][:CHUNK] * dec[u], 0.0)
            tinv[u] = eye_p + pw[u]
        for u in units:
            pw[u] = _dot(pw[u], block_diag(pw[u]))
        for _ in range(4):
            r = {u: _dot(jnp.concatenate([tinv[u], pw[u]], axis=0), block_diag(pw[u])) for u in units}
            for u in units:
                tinv[u] = tinv[u] + r[u][:CHUNK]
                pw[u] = r[u][CHUNK:]
        for u in units:
            tinv[u] = tinv[u] + _dot(tinv[u], block_diag(pw[u]))
        uw = {}
        for (cc, p) in units:
            for j, h in enumerate((2 * p, 2 * p + 1)):
                th = tinv[cc, p][:, j * CHUNK:(j + 1) * CHUNK]
                eg = jnp.exp(gb_s[rows[cc], hs[h]])
                kbh = kb[cc, p][:, j * LANES:(j + 1) * LANES]
                uw[cc, h] = _dot(th, jnp.concatenate(
                    [v_at(rows[cc], hs[h]) * bb_s[rows[cc], hs[h]], kbh * eg], axis=1))
        for cc in range(nchunk):
            c = cidx[cc]
            for h in hr:
                gi = gb_s[rows[cc], hs[h]]
                glast = gb_s[last[cc], hs[h]]
                u_s[slot, c, h] = uw[cc, h][:, :DN_DV]
                wq_s[slot, c, h] = jnp.concatenate([uw[cc, h][:, DN_DV:], q_at(rows[cc], hs[h]) * jnp.exp(gi)],
                                                   axis=0).astype(BF16)
                kd_s[slot, c, h] = (k_at(rows[cc], hs[h]) * jnp.exp(glast - gi)).astype(BF16)
                el_s[slot, c, h:h + 1, :] = jnp.exp(glast)

    def scan_chunk(c):
        rows = pl.ds(c * CHUNK, CHUNK)
        s_old = [s_ref[h] for h in hr]
        sw = [_dot(wq_s[prev, c, h], s_old[h]) for h in hr]
        v_new = [u_s[prev, c, h] - sw[h][:CHUNK] for h in hr]
        o = [sw[h][CHUNK:] + _dot(at_s[prev, c, h], v_new[h]) for h in hr]
        for h in hr:
            s_ref[h] = s_old[h] * el_s[prev, c, h:h + 1, :] + _dot_tn(kd_s[prev, c, h], v_new[h])
        for h in hr:
            z = z_ref[rows, hs[h]].astype(F32)
            y = (o[h] * lax.rsqrt(jnp.mean(o[h] * o[h], -1, keepdims=True) + RMS_EPS)
                 * onw * _silu(z))
            y_ref[rows, hs[h]] = y.astype(y_ref.dtype)

    for c in range(nchunk):
        scan_chunk(c)
    prep_chunks()


def _deltanet(qkv, rest, x, w_small, apar, onw, batch, seq, d, ts):
    t_total = x.shape[0]
    nt = seq // ts
    nc = ts // CHUNK
    kern = functools.partial(_deltanet_kernel, ts=ts, d=d)

    def cur(b, t):
        return b * nt + jnp.minimum(t, nt - 1)

    def lag(b, t):
        return b * nt + jnp.maximum(t - 1, 0)

    return pl.pallas_call(
        kern,
        grid=(batch, nt + 1),
        in_specs=[
            pl.BlockSpec((ts, 3 * d), lambda b, t: (cur(b, t), 0)),
            pl.BlockSpec((ts, d), lambda b, t: (lag(b, t), SEC_Z)),
            pl.BlockSpec((ts, d), lambda b, t: (cur(b, t), 0)),
            pl.BlockSpec((d, LANES), lambda b, t: (0, 0)),
            pl.BlockSpec((2, LANES), lambda b, t: (0, 0)),
            pl.BlockSpec((1, LANES), lambda b, t: (0, 0)),
        ],
        out_specs=pl.BlockSpec((ts, d), lambda b, t: (lag(b, t), 0)),
        out_shape=jax.ShapeDtypeStruct((t_total, d), BF16),
        scratch_shapes=[
            pltpu.VMEM((DN_HEADS, DN_DK, DN_DV), F32),
            pltpu.VMEM((ts, d), F32),
            pltpu.VMEM((ts, d), F32),
            pltpu.VMEM((2, nc, DN_HEADS, CHUNK, DN_DV), F32),
            pltpu.VMEM((2, nc, DN_HEADS, 2 * CHUNK, DN_DK), BF16),
            pltpu.VMEM((2, nc, DN_HEADS, CHUNK, CHUNK), BF16),
            pltpu.VMEM((2, nc, DN_HEADS, CHUNK, DN_DK), BF16),
            pltpu.VMEM((2, nc, DN_HEADS, LANES), F32),
        ],
        compiler_params=pltpu.CompilerParams(
            dimension_semantics=("arbitrary", "arbitrary"),
            vmem_limit_bytes=VMEM_LIMIT),
        name="deltanet",
    )(qkv, rest, x, w_small, apar, onw)


def _spatial_gate(u_ref, vg_ref, ws_ref, bst_ref, lng, lnb, tm):
    ri = lax.broadcasted_iota(jnp.int32, (SGU_BLOCK, SGU_BLOCK), 0) // CHUNK
    ci = lax.broadcasted_iota(jnp.int32, (SGU_BLOCK, SGU_BLOCK), 1) // CHUNK
    causal = ri >= ci
    ws = [jnp.where(causal, ws_ref[g], 0.0).astype(BF16) for g in range(SGU_GROUPS)]
    blocks = []
    for n in range(tm // SGU_BLOCK):
        rows = slice(n * SGU_BLOCK, (n + 1) * SGU_BLOCK)
        vn = _layer_norm(vg_ref[rows, :].astype(F32), lng, lnb).astype(BF16)
        parts = []
        for g in range(SGU_GROUPS):
            cs = slice(g * SGU_GROUP_DIM, (g + 1) * SGU_GROUP_DIM)
            sp = jnp.dot(ws[g], vn[:, cs], preferred_element_type=F32) + bst_ref[:, g:g + 1]
            parts.append((u_ref[rows, cs].astype(F32) * sp).astype(BF16))
        blocks.append(jnp.concatenate(parts, axis=1))
    return jnp.concatenate(blocks, axis=0)


def _merge_kernel(ya_ref, u_ref, vg_ref, ga_ref, gb_ref, x_ref, ws_ref, bst_ref, lng_ref, lnb_ref,
                  wpa_ref, wpb_ref, wo_ref, g_ref, b_ref, o_ref, *, alpha, tm):
    yb = _spatial_gate(u_ref, vg_ref, ws_ref, bst_ref, lng_ref[...], lnb_ref[...], tm)
    pa = jnp.dot(ya_ref[...], wpa_ref[...], preferred_element_type=F32)
    pb = jnp.dot(yb, wpb_ref[...], preferred_element_type=F32)
    m = (jax.nn.sigmoid(ga_ref[...].astype(F32)) * pa
         + jax.nn.sigmoid(gb_ref[...].astype(F32)) * pb)
    mix = jnp.dot(m.astype(BF16), wo_ref[...], preferred_element_type=F32)
    o_ref[...] = _layer_norm(alpha * x_ref[...] + mix, g_ref[...], b_ref[...])


def _merge(ya, rest, x, w_s, b_s_t, sgu_g, sgu_b, wpa, wpb, wo, ln_g, ln_b, layer, alpha, tm):
    t_total, d = x.shape
    assert tm % SGU_BLOCK == 0
    kern = functools.partial(_merge_kernel, alpha=alpha, tm=tm)
    row = pl.BlockSpec((tm, d), lambda i: (i, 0))

    def sec(s):
        return pl.BlockSpec((tm, d), lambda i: (i, s))

    return pl.pallas_call(
        kern,
        grid=(t_total // tm,),
        in_specs=[row, sec(SEC_U), sec(SEC_VG), sec(SEC_GA), sec(SEC_GB), row,
                  _resident(w_s.shape[1:], layer), _resident(b_s_t.shape[1:], layer),
                  _resident((1, d)), _resident((1, d)),
                  _resident((d, d), layer), _resident((d, d), layer), _resident((d, d), layer),
                  _resident((1, d)), _resident((1, d))],
        out_specs=row,
        out_shape=jax.ShapeDtypeStruct((t_total, d), F32),
        compiler_params=pltpu.CompilerParams(
            dimension_semantics=("arbitrary",), vmem_limit_bytes=VMEM_LIMIT),
        name="merge",
    )(ya, rest, rest, rest, rest, x, w_s, b_s_t, sgu_g, sgu_b, wpa, wpb, wo, ln_g, ln_b)


def _ffn_kernel(x_ref, wg_ref, wu_ref, wd_ref, g_ref, b_ref, o_ref, *, alpha):
    xb = x_ref[...].astype(BF16)
    gate = jnp.dot(xb, wg_ref[...], preferred_element_type=F32)
    up = jnp.dot(xb, wu_ref[...], preferred_element_type=F32)
    hid = (_silu(gate) * up).astype(BF16)
    ffn = jnp.dot(hid, wd_ref[...], preferred_element_type=F32)
    o_ref[...] = _layer_norm(alpha * x_ref[...] + ffn, g_ref[...], b_ref[...])


def _ffn(x, wg, wu, wd, ln_g, ln_b, layer, alpha, tm):
    t_total, d = x.shape
    hdim = wg.shape[2]
    kern = functools.partial(_ffn_kernel, alpha=alpha)
    row = pl.BlockSpec((tm, d), lambda i: (i, 0))
    return pl.pallas_call(
        kern,
        grid=(t_total // tm,),
        in_specs=[row,
                  _resident((d, hdim), layer), _resident((d, hdim), layer),
                  _resident((hdim, d), layer), _resident((1, d)), _resident((1, d))],
        out_specs=row,
        out_shape=jax.ShapeDtypeStruct((t_total, d), F32),
        compiler_params=pltpu.CompilerParams(
            dimension_semantics=("arbitrary",), vmem_limit_bytes=VMEM_LIMIT),
        name="ffn",
    )(x, wg, wu, wd, ln_g, ln_b)


def _pick_tile(n, want):
    tile = min(n, want)
    assert n % tile == 0, (n, tile)
    return tile


def kernel(x, w_in, conv_w, a_log, dt_bias, o_norm_w, sgu_ln_g, sgu_ln_b, w_s, b_s, w_pa, w_pb, w_o, ln1_g, ln1_b, w_ffn_gate, w_ffn_up, w_ffn_down, ln2_g, ln2_b):
    batch, seq, d = x.shape
    depth = w_in.shape[0]
    assert d == DN_HEADS * DN_DK == SGU_GROUPS * SGU_GROUP_DIM
    assert seq % SGU_BLOCK == 0
    alpha = (2 * depth) ** 0.25
    t_total = batch * seq

    xf = x.reshape(t_total, d)
    nsm = 2 * DN_HEADS
    w_head = w_in[:, :, :4 * d].astype(BF16)
    w_tail = w_in[:, :, 4 * d + nsm:].astype(BF16)
    w_small = jnp.pad(w_in[:, :, 4 * d:4 * d + nsm], ((0, 0), (0, 0), (0, LANES - nsm))).astype(BF16)
    wpa, wpb, wo = w_pa.astype(BF16), w_pb.astype(BF16), w_o.astype(BF16)
    wg, wu, wd = w_ffn_gate.astype(BF16), w_ffn_up.astype(BF16), w_ffn_down.astype(BF16)
    b_s_t = jnp.swapaxes(b_s, 1, 2)
    lane_pad = ((0, 0), (DN_HEADS, LANES - nsm))
    apar = jnp.stack([jnp.pad(a_log, lane_pad), jnp.pad(dt_bias, lane_pad)], axis=1)
    for l in range(depth):
        qkv, rest = _in_proj(xf, w_head, w_tail, conv_w, l, seq, _pick_tile(seq, 512))
        ya = _deltanet(qkv, rest, xf, w_small[l], apar[l], o_norm_w[l].reshape(1, DN_DV),
                       batch, seq, d, _pick_tile(seq, 256))
        xf = _merge(ya, rest, xf, w_s, b_s_t, sgu_ln_g[l].reshape(1, d), sgu_ln_b[l].reshape(1, d),
                    wpa, wpb, wo, ln1_g[l].reshape(1, d), ln1_b[l].reshape(1, d),
                    l, alpha, _pick_tile(t_total, 512))
        xf = _ffn(xf, wg, wu, wd, ln2_g[l].reshape(1, d), ln2_b[l].reshape(1, d),
                  l, alpha, _pick_tile(t_total, 256))
    return xf.reshape(batch, seq, d)
```

```python
import functools
import math

import jax
import jax.numpy as jnp
from jax import lax
from jax.experimental import pallas as pl
from jax.experimental.pallas import tpu as pltpu

F32 = jnp.float32
BF16 = jnp.bfloat16

CHUNK = 64
DN_HEADS = 8
DN_DK = 128
DN_DV = 128
CONV_K = 4
SGU_BLOCK = 128
SGU_GROUPS = 8
SGU_GROUP_DIM = 128
LN_EPS = 1e-5
RMS_EPS = 1e-6
LOG2E = math.log2(math.e)
CONV_HIST = 16
CONV_BUFS = 4

LANES = 128
SUBLANES = 8
MXU_COLS = 256
REST_COLS = 512
VMEM_LIMIT = 56 * 1024 * 1024

SEC_Q, SEC_K, SEC_V = range(3)
SEC_Z, SEC_U, SEC_VG, SEC_GA, SEC_GB = range(5)


def _layer_norm(h, g, b):
    mu = jnp.mean(h, -1, keepdims=True)
    d = h - mu
    var = jnp.mean(d * d, -1, keepdims=True)
    return d * lax.rsqrt(var + LN_EPS) * g + b


def _gelu(x):
    return 0.5 * x * (1.0 + lax.erf(x * (1.0 / math.sqrt(2.0))))


def _silu(x):
    return x / (1.0 + jnp.exp2(x * (-LOG2E)))


def _dot(a, b):
    return jnp.dot(a.astype(BF16), b.astype(BF16), preferred_element_type=F32)


def _dot_nt(a, b):
    return lax.dot_general(a.astype(BF16), b.astype(BF16), (((1,), (1,)), ((), ())),
                           preferred_element_type=F32)


def _dot_tn(a, b):
    return lax.dot_general(a.astype(BF16), b.astype(BF16), (((0,), (0,)), ((), ())),
                           preferred_element_type=F32)


def _resident(shape, layer=None):
    if layer is None:
        return pl.BlockSpec(shape, lambda i: (0,) * len(shape), pipeline_mode=pl.Buffered(1))
    return pl.BlockSpec((None,) + tuple(shape), lambda i: (layer,) + (0,) * len(shape),
                        pipeline_mode=pl.Buffered(1))


def _in_proj_kernel(x_ref, xh_ref, whead_ref, wtail_ref, cw_ref, qkv_ref, rest_ref, stage,
                    *, tm, d, tiles_per_seq):
    i = pl.program_id(0)
    keep_hist = (i % tiles_per_seq != 0).astype(F32)
    xb = x_ref[...].astype(BF16)
    xx = jnp.concatenate([xh_ref[...].astype(BF16), xb], axis=0)
    nrest = rest_ref.shape[1]
    assert d % REST_COLS == 0 and nrest // REST_COLS <= 3 * d // MXU_COLS
    for c0 in range(0, 3 * d, MXU_COLS):
        sec = c0 // d
        p = jnp.dot(xx, whead_ref[:, c0:c0 + MXU_COLS], preferred_element_type=F32)
        for s0 in range(0, MXU_COLS, LANES):
            cols = slice(c0 + s0, c0 + s0 + LANES)
            buf = stage.at[(c0 + s0) // LANES % CONV_BUFS]
            buf[0:CONV_HIST, :] = p[0:CONV_HIST, s0:s0 + LANES] * keep_hist
            buf[CONV_HIST:CONV_HIST + tm, :] = p[CONV_HIST:, s0:s0 + LANES]
            w = cw_ref[:, cols]
            acc = buf[CONV_HIST:CONV_HIST + tm, :] * w[CONV_K - 1:CONV_K]
            for j in range(CONV_K - 1):
                off = CONV_HIST - (CONV_K - 1) + j
                acc = acc + buf[off:off + tm, :] * w[j:j + 1]
            y = _silu(acc)
            if sec != SEC_V:
                scale = DN_DK ** -0.5 if sec == SEC_Q else 1.0
                y = y * (lax.rsqrt(jnp.sum(y * y, -1, keepdims=True) + RMS_EPS) * scale)
            qkv_ref[:, cols] = y
        g = c0 // MXU_COLS
        if g < nrest // REST_COLS:
            r0 = g * REST_COLS
            rsec = r0 // d
            if rsec == SEC_Z:
                w = whead_ref[:, 3 * d + r0:3 * d + r0 + REST_COLS]
            else:
                w = wtail_ref[:, r0 - d:r0 - d + REST_COLS]
            r = jnp.dot(xb, w, preferred_element_type=F32)
            if rsec in (SEC_U, SEC_VG):
                r = _gelu(r)
            rest_ref[:, r0:r0 + REST_COLS] = r.astype(rest_ref.dtype)


def _in_proj(x, w_all, w_tail, convw, layer, seq, tm):
    t, d = x.shape
    nrest = d + w_tail.shape[2]
    hist_blocks = tm // CONV_HIST
    kern = functools.partial(_in_proj_kernel, tm=tm, d=d, tiles_per_seq=seq // tm)
    return pl.pallas_call(
        kern,
        grid=(t // tm,),
        in_specs=[pl.BlockSpec((tm, d), lambda i: (i, 0)),
                  pl.BlockSpec((CONV_HIST, d), lambda i: (jnp.maximum(i * hist_blocks - 1, 0), 0)),
                  _resident((d, 4 * d), layer), _resident(w_tail.shape[1:], layer),
                  _resident(convw.shape[1:], layer)],
        out_specs=[pl.BlockSpec((tm, 3 * d), lambda i: (i, 0)),
                   pl.BlockSpec((tm, nrest), lambda i: (i, 0))],
        out_shape=[jax.ShapeDtypeStruct((t, 3 * d), F32),
                   jax.ShapeDtypeStruct((t, nrest), BF16)],
        scratch_shapes=[pltpu.VMEM((CONV_BUFS, CONV_HIST + tm, LANES), F32)],
        compiler_params=pltpu.CompilerParams(
            dimension_semantics=("arbitrary",), vmem_limit_bytes=VMEM_LIMIT),
        name="in_proj",
    )(x, x, w_all, w_tail, convw)


def _deltanet_kernel(qkv_ref, z_ref, x_ref, wsm_ref, apar_ref, onw_ref,
                     y_ref, s_ref, gb_s, bb_s, u_s, wq_s, at_s, kd_s, el_s, *, nb, ts, d):
    t = pl.program_id(0)
    heads = DN_HEADS
    slot = t % 2
    prev = 1 - slot

    @pl.when(t == 0)
    def _():
        s_ref[...] = jnp.zeros_like(s_ref)
        u_s[1] = jnp.zeros_like(u_s[1])
        wq_s[1] = jnp.zeros_like(wq_s[1])
        at_s[1] = jnp.zeros_like(at_s[1])
        kd_s[1] = jnp.zeros_like(kd_s[1])
        el_s[1] = jnp.zeros_like(el_s[1])

    def q_at(b, rows, cols):
        return qkv_ref[b, rows, SEC_Q * d + cols.start:SEC_Q * d + cols.stop]

    def k_at(b, rows, cols):
        return qkv_ref[b, rows, SEC_K * d + cols.start:SEC_K * d + cols.stop]

    def v_at(b, rows, cols):
        return qkv_ref[b, rows, SEC_V * d + cols.start:SEC_V * d + cols.stop]

    hr = range(heads)
    hs = [slice(h * LANES, (h + 1) * LANES) for h in hr]
    pairs = range(heads // 2)
    nchunk = ts // CHUNK
    onw = onw_ref[...]

    tr = lax.broadcasted_iota(jnp.int32, (ts, ts), 0)
    tc = lax.broadcasted_iota(jnp.int32, (ts, ts), 1)
    tril_bd = ((tr // CHUNK == tc // CHUNK) & (tr >= tc)).astype(BF16)
    for b in range(nb):
        sm = jnp.dot(x_ref[b].astype(BF16), wsm_ref[...], preferred_element_type=F32)
        beta_all = jax.nn.sigmoid(sm)
        g_all = -jnp.exp(apar_ref[0:1, :]) * jax.nn.softplus(sm + apar_ref[1:2, :])
        g_hi = g_all.astype(BF16)
        g_mid = (g_all - g_hi.astype(F32)).astype(BF16)
        g_lo = (g_all - g_hi.astype(F32) - g_mid.astype(F32)).astype(BF16)
        gcum_all = (jnp.dot(tril_bd, g_hi, preferred_element_type=F32)
                    + jnp.dot(tril_bd, g_mid, preferred_element_type=F32)
                    + jnp.dot(tril_bd, g_lo, preferred_element_type=F32))
        for h in hr:
            gb_s[b, :, hs[h]] = jnp.broadcast_to(gcum_all[:, heads + h:heads + h + 1], (ts, LANES))
            bb_s[b, :, hs[h]] = jnp.broadcast_to(beta_all[:, h:h + 1], (ts, LANES))

    prow = lax.broadcasted_iota(jnp.int32, (CHUNK, LANES), 0)
    plane = lax.broadcasted_iota(jnp.int32, (CHUNK, LANES), 1)
    pcol = jnp.where(plane < CHUNK, plane, plane - CHUNK)
    left = plane < CHUNK
    tril_p = prow >= pcol
    strict_p = prow > pcol
    eye_b = prow == pcol
    eye_p = eye_b.astype(F32)
    first_head = lax.broadcasted_iota(jnp.int32, (CHUNK, 2 * LANES), 1) < LANES

    def block_diag(p):
        return jnp.concatenate([jnp.where(left, p, 0.0), jnp.where(left, 0.0, p)], axis=0)

    def prep_chunks():
        units = [(b, c, p) for b in range(nb) for c in range(nchunk) for p in pairs]
        rows = [pl.ds(c * CHUNK, CHUNK) for c in range(nchunk)]
        last = [pl.ds((c + 1) * CHUNK - 1, 1) for c in range(nchunk)]

        a, dec, kb = {}, {}, {}
        for u in units:
            b, c, p = u
            h1, h2 = 2 * p, 2 * p + 1
            cols = slice(h1 * LANES, (h2 + 1) * LANES)
            gi = jnp.where(left, gb_s[b, rows[c], hs[h1]], gb_s[b, rows[c], hs[h2]])
            gj = jnp.sum(jnp.where(eye_b, gi, 0.0), axis=0, keepdims=True)
            dec[u] = jnp.where(tril_p, jnp.exp(jnp.minimum(gi - gj, 0.0)), 0.0)
            k2 = k_at(b, rows[c], cols)
            kb[u] = k2 * bb_s[b, rows[c], cols]
            lhs = jnp.concatenate([kb[u], q_at(b, rows[c], cols)], axis=0)
            rhs = jnp.concatenate([jnp.where(first_head, k2, 0.0),
                                   jnp.where(first_head, 0.0, k2)], axis=0)
            a[u] = _dot_nt(lhs, rhs)
        pw, tinv = {}, {}
        for u in units:
            b, c, p = u
            attn = a[u][CHUNK:] * dec[u]
            at_s[slot, b, c, 2 * p] = attn[:, :CHUNK].astype(BF16)
            at_s[slot, b, c, 2 * p + 1] = attn[:, CHUNK:].astype(BF16)
            pw[u] = -jnp.where(strict_p, a[u][:CHUNK] * dec[u], 0.0)
            tinv[u] = eye_p + pw[u]
        for u in units:
            pw[u] = _dot(pw[u], block_diag(pw[u]))
        for _ in range(4):
            r = {u: _dot(jnp.concatenate([tinv[u], pw[u]], axis=0), block_diag(pw[u])) for u in units}
            for u in units:
                tinv[u] = tinv[u] + r[u][:CHUNK]
                pw[u] = r[u][CHUNK:]
        for u in units:
            tinv[u] = tinv[u] + _dot(tinv[u], block_diag(pw[u]))
        for u in units:
            b, c, p = u
            for j, h in enumerate((2 * p, 2 * p + 1)):
                th = tinv[u][:, j * CHUNK:(j + 1) * CHUNK]
                gi = gb_s[b, rows[c], hs[h]]
                glast = gb_s[b, last[c], hs[h]]
                eg = jnp.exp(gi)
                kbh = kb[u][:, j * LANES:(j + 1) * LANES]
                uw = _dot(th, jnp.concatenate(
                    [v_at(b, rows[c], hs[h]) * bb_s[b, rows[c], hs[h]], kbh * eg], axis=1))
                u_s[slot, b, c, h] = uw[:, :DN_DV]
                wq_s[slot, b, c, h] = jnp.concatenate(
                    [uw[:, DN_DV:], q_at(b, rows[c], hs[h]) * eg], axis=0).astype(BF16)
                kd_s[slot, b, c, h] = (k_at(b, rows[c], hs[h]) * jnp.exp(glast - gi)).astype(BF16)
                el_s[slot, b, c, h:h + 1, :] = jnp.exp(glast)

    def scan_chunk(c):
        rows = pl.ds(c * CHUNK, CHUNK)
        chains = [(b, h) for b in range(nb) for h in hr]
        s_old = {bh: s_ref[bh[0], bh[1]] for bh in chains}
        sw = {(b, h): _dot(wq_s[prev, b, c, h], s_old[b, h]) for (b, h) in chains}
        v_new = {(b, h): u_s[prev, b, c, h] - sw[b, h][:CHUNK] for (b, h) in chains}
        o = {(b, h): sw[b, h][CHUNK:] + _dot(at_s[prev, b, c, h], v_new[b, h]) for (b, h) in chains}
        for (b, h) in chains:
            s_ref[b, h] = (s_old[b, h] * el_s[prev, b, c, h:h + 1, :]
                           + _dot_tn(kd_s[prev, b, c, h], v_new[b, h]))
        for (b, h) in chains:
            z = z_ref[b, rows, hs[h]].astype(F32)
            ob = o[b, h]
            y = ob * lax.rsqrt(jnp.mean(ob * ob, -1, keepdims=True) + RMS_EPS) * onw * _silu(z)
            y_ref[b, rows, hs[h]] = y.astype(y_ref.dtype)

    for c in range(nchunk):
        scan_chunk(c)
    prep_chunks()


def _deltanet(qkv, rest, x, w_small, apar, onw, batch, seq, d, ts):
    nt = seq // ts
    nc = ts // CHUNK
    kern = functools.partial(_deltanet_kernel, nb=batch, ts=ts, d=d)

    def cur(t):
        return jnp.minimum(t, nt - 1)

    def lag(t):
        return jnp.maximum(t - 1, 0)

    y = pl.pallas_call(
        kern,
        grid=(nt + 1,),
        in_specs=[
            pl.BlockSpec((batch, ts, 3 * d), lambda t: (0, cur(t), 0)),
            pl.BlockSpec((batch, ts, d), lambda t: (0, lag(t), SEC_Z)),
            pl.BlockSpec((batch, ts, d), lambda t: (0, cur(t), 0)),
            pl.BlockSpec((d, LANES), lambda t: (0, 0)),
            pl.BlockSpec((2, LANES), lambda t: (0, 0)),
            pl.BlockSpec((1, LANES), lambda t: (0, 0)),
        ],
        out_specs=pl.BlockSpec((batch, ts, d), lambda t: (0, lag(t), 0)),
        out_shape=jax.ShapeDtypeStruct((batch, seq, d), BF16),
        scratch_shapes=[
            pltpu.VMEM((batch, DN_HEADS, DN_DK, DN_DV), F32),
            pltpu.VMEM((batch, ts, d), F32),
            pltpu.VMEM((batch, ts, d), F32),
            pltpu.VMEM((2, batch, nc, DN_HEADS, CHUNK, DN_DV), F32),
            pltpu.VMEM((2, batch, nc, DN_HEADS, 2 * CHUNK, DN_DK), BF16),
            pltpu.VMEM((2, batch, nc, DN_HEADS, CHUNK, CHUNK), BF16),
            pltpu.VMEM((2, batch, nc, DN_HEADS, CHUNK, DN_DK), BF16),
            pltpu.VMEM((2, batch, nc, DN_HEADS, LANES), F32),
        ],
        compiler_params=pltpu.CompilerParams(
            dimension_semantics=("arbitrary",), vmem_limit_bytes=VMEM_LIMIT),
        name="deltanet",
    )(qkv.reshape(batch, seq, 3 * d), rest.reshape(batch, seq, -1), x.reshape(batch, seq, d),
      w_small, apar, onw)
    return y.reshape(batch * seq, d)


def _spatial_gate(u_ref, vg_ref, ws_ref, bst_ref, lng, lnb, tm):
    ri = lax.broadcasted_iota(jnp.int32, (SGU_BLOCK, SGU_BLOCK), 0) // CHUNK
    ci = lax.broadcasted_iota(jnp.int32, (SGU_BLOCK, SGU_BLOCK), 1) // CHUNK
    causal = ri >= ci
    ws = [jnp.where(causal, ws_ref[g], 0.0).astype(BF16) for g in range(SGU_GROUPS)]
    blocks = []
    for n in range(tm // SGU_BLOCK):
        rows = slice(n * SGU_BLOCK, (n + 1) * SGU_BLOCK)
        vn = _layer_norm(vg_ref[rows, :].astype(F32), lng, lnb).astype(BF16)
        parts = []
        for g in range(SGU_GROUPS):
            cs = slice(g * SGU_GROUP_DIM, (g + 1) * SGU_GROUP_DIM)
            sp = jnp.dot(ws[g], vn[:, cs], preferred_element_type=F32) + bst_ref[:, g:g + 1]
            parts.append((u_ref[rows, cs].astype(F32) * sp).astype(BF16))
        blocks.append(jnp.concatenate(parts, axis=1))
    return jnp.concatenate(blocks, axis=0)


def _merge_kernel(ya_ref, u_ref, vg_ref, ga_ref, gb_ref, x_ref, ws_ref, bst_ref, lng_ref, lnb_ref,
                  wpa_ref, wpb_ref, wo_ref, g_ref, b_ref, o_ref, *, alpha, tm):
    yb = _spatial_gate(u_ref, vg_ref, ws_ref, bst_ref, lng_ref[...], lnb_ref[...], tm)
    pa = jnp.dot(ya_ref[...], wpa_ref[...], preferred_element_type=F32)
    pb = jnp.dot(yb, wpb_ref[...], preferred_element_type=F32)
    m = (jax.nn.sigmoid(ga_ref[...].astype(F32)) * pa
         + jax.nn.sigmoid(gb_ref[...].astype(F32)) * pb)
    mix = jnp.dot(m.astype(BF16), wo_ref[...], preferred_element_type=F32)
    o_ref[...] = _layer_norm(alpha * x_ref[...] + mix, g_ref[...], b_ref[...])


def _merge(ya, rest, x, w_s, b_s_t, sgu_g, sgu_b, wpa, wpb, wo, ln_g, ln_b, layer, alpha, tm):
    t_total, d = x.shape
    assert tm % SGU_BLOCK == 0
    kern = functools.partial(_merge_kernel, alpha=alpha, tm=tm)
    row = pl.BlockSpec((tm, d), lambda i: (i, 0))

    def sec(s):
        return pl.BlockSpec((tm, d), lambda i: (i, s))

    return pl.pallas_call(
        kern,
        grid=(t_total // tm,),
        in_specs=[row, sec(SEC_U), sec(SEC_VG), sec(SEC_GA), sec(SEC_GB), row,
                  _resident(w_s.shape[1:], layer), _resident(b_s_t.shape[1:], layer),
                  _resident((1, d)), _resident((1, d)),
                  _resident((d, d), layer), _resident((d, d), layer), _resident((d, d), layer),
                  _resident((1, d)), _resident((1, d))],
        out_specs=row,
        out_shape=jax.ShapeDtypeStruct((t_total, d), F32),
        compiler_params=pltpu.CompilerParams(
            dimension_semantics=("arbitrary",), vmem_limit_bytes=VMEM_LIMIT),
        name="merge",
    )(ya, rest, rest, rest, rest, x, w_s, b_s_t, sgu_g, sgu_b, wpa, wpb, wo, ln_g, ln_b)


def _ffn_kernel(x_ref, wg_ref, wu_ref, wd_ref, g_ref, b_ref, o_ref, *, alpha):
    xb = x_ref[...].astype(BF16)
    gate = jnp.dot(xb, wg_ref[...], preferred_element_type=F32)
    up = jnp.dot(xb, wu_ref[...], preferred_element_type=F32)
    hid = (_silu(gate) * up).astype(BF16)
    ffn = jnp.dot(hid, wd_ref[...], preferred_element_type=F32)
    o_ref[...] = _layer_norm(alpha * x_ref[...] + ffn, g_ref[...], b_ref[...])


def _ffn(x, wg, wu, wd, ln_g, ln_b, layer, alpha, tm):
    t_total, d = x.shape
    hdim = wg.shape[2]
    kern = functools.partial(_ffn_kernel, alpha=alpha)
    row = pl.BlockSpec((tm, d), lambda i: (i, 0))
    return pl.pallas_call(
        kern,
        grid=(t_total // tm,),
        in_specs=[row,
                  _resident((d, hdim), layer), _resident((d, hdim), layer),
                  _resident((hdim, d), layer), _resident((1, d)), _resident((1, d))],
        out_specs=row,
        out_shape=jax.ShapeDtypeStruct((t_total, d), F32),
        compiler_params=pltpu.CompilerParams(
            dimension_semantics=("arbitrary",), vmem_limit_bytes=VMEM_LIMIT),
        name="ffn",
    )(x, wg, wu, wd, ln_g, ln_b)


def _pick_tile(n, want):
    tile = min(n, want)
    assert n % tile == 0, (n, tile)
    return tile


def kernel(x, w_in, conv_w, a_log, dt_bias, o_norm_w, sgu_ln_g, sgu_ln_b, w_s, b_s, w_pa, w_pb, w_o, ln1_g, ln1_b, w_ffn_gate, w_ffn_up, w_ffn_down, ln2_g, ln2_b):
    batch, seq, d = x.shape
    depth = w_in.shape[0]
    assert d == DN_HEADS * DN_DK == SGU_GROUPS * SGU_GROUP_DIM
    assert seq % SGU_BLOCK == 0
    alpha = (2 * depth) ** 0.25
    t_total = batch * seq

    xf = x.reshape(t_total, d)
    nsm = 2 * DN_HEADS
    w_all = w_in.astype(BF16)
    w_tail = w_all[:, :, 4 * d + nsm:]
    w_small = jnp.pad(w_all[:, :, 4 * d:4 * d + nsm], ((0, 0), (0, 0), (0, LANES - nsm)))
    wpa, wpb, wo = w_pa.astype(BF16), w_pb.astype(BF16), w_o.astype(BF16)
    wg, wu, wd = w_ffn_gate.astype(BF16), w_ffn_up.astype(BF16), w_ffn_down.astype(BF16)
    b_s_t = jnp.swapaxes(b_s, 1, 2)
    lane_pad = ((0, 0), (DN_HEADS, LANES - nsm))
    apar = jnp.stack([jnp.pad(a_log, lane_pad), jnp.pad(dt_bias, lane_pad)], axis=1)
    for l in range(depth):
        qkv, rest = _in_proj(xf, w_all, w_tail, conv_w, l, seq, _pick_tile(seq, 512))
        ya = _deltanet(qkv, rest, xf, w_small[l], apar[l], o_norm_w[l].reshape(1, DN_DV),
                       batch, seq, d, _pick_tile(seq, 256))
        xf = _merge(ya, rest, xf, w_s, b_s_t, sgu_ln_g[l].reshape(1, d), sgu_ln_b[l].reshape(1, d),
                    wpa, wpb, wo, ln1_g[l].reshape(1, d), ln1_b[l].reshape(1, d),
                    l, alpha, _pick_tile(t_total, 1024))
        xf = _ffn(xf, wg, wu, wd, ln2_g[l].reshape(1, d), ln2_b[l].reshape(1, d),
                  l, alpha, _pick_tile(t_total, 512))
    return xf.reshape(batch, seq, d)
```

```python
import functools
import math

import jax
import jax.numpy as jnp
from jax import lax
from jax.experimental import pallas as pl
from jax.experimental.pallas import tpu as pltpu

F32 = jnp.float32
BF16 = jnp.bfloat16

CHUNK = 64
DN_HEADS = 8
DN_DK = 128
DN_DV = 128
CONV_K = 4
SGU_BLOCK = 128
SGU_GROUPS = 8
SGU_GROUP_DIM = 128
LN_EPS = 1e-5
RMS_EPS = 1e-6
LOG2E = math.log2(math.e)
CONV_HIST = 16
CONV_BUFS = 4

LANES = 128
SUBLANES = 8
MXU_COLS = 256
REST_COLS = 512
VMEM_LIMIT = 56 * 1024 * 1024

SEC_Q, SEC_K, SEC_V = range(3)
SEC_Z, SEC_U, SEC_VG, SEC_GA, SEC_GB = range(5)


def _layer_norm(h, g, b):
    mu = jnp.mean(h, -1, keepdims=True)
    d = h - mu
    var = jnp.mean(d * d, -1, keepdims=True)
    return d * lax.rsqrt(var + LN_EPS) * g + b


def _gelu(x):
    return 0.5 * x * (1.0 + lax.erf(x * (1.0 / math.sqrt(2.0))))


def _silu(x):
    return x / (1.0 + jnp.exp2(x * (-LOG2E)))


def _dot(a, b):
    return jnp.dot(a.astype(BF16), b.astype(BF16), preferred_element_type=F32)


def _dot_nt(a, b):
    return lax.dot_general(a.astype(BF16), b.astype(BF16), (((1,), (1,)), ((), ())),
                           preferred_element_type=F32)


def _dot_tn(a, b):
    return lax.dot_general(a.astype(BF16), b.astype(BF16), (((0,), (0,)), ((), ())),
                           preferred_element_type=F32)


def _resident(shape, layer=None):
    if layer is None:
        return pl.BlockSpec(shape, lambda i: (0,) * len(shape), pipeline_mode=pl.Buffered(1))
    return pl.BlockSpec((None,) + tuple(shape), lambda i: (layer,) + (0,) * len(shape),
                        pipeline_mode=pl.Buffered(1))


def _in_proj_kernel(x_ref, xh_ref, wt_ref, cw_ref, qkv_ref, rest_ref, stage,
                    *, tm, d, tail0, tiles_per_seq):
    i = pl.program_id(0)
    keep_hist = (i % tiles_per_seq != 0).astype(F32)
    xb = x_ref[...].astype(BF16)
    xx = jnp.concatenate([xh_ref[...].astype(BF16), xb], axis=0)
    nrest = rest_ref.shape[1]
    assert d % REST_COLS == 0 and nrest // REST_COLS <= 3 * d // MXU_COLS
    for c0 in range(0, 3 * d, MXU_COLS):
        sec = c0 // d
        p = _dot_nt(xx, wt_ref[c0:c0 + MXU_COLS, :])
        for s0 in range(0, MXU_COLS, LANES):
            cols = slice(c0 + s0, c0 + s0 + LANES)
            buf = stage.at[(c0 + s0) // LANES % CONV_BUFS]
            buf[0:CONV_HIST, :] = p[0:CONV_HIST, s0:s0 + LANES] * keep_hist
            buf[CONV_HIST:CONV_HIST + tm, :] = p[CONV_HIST:, s0:s0 + LANES]
            w = cw_ref[:, cols]
            acc = buf[CONV_HIST:CONV_HIST + tm, :] * w[CONV_K - 1:CONV_K]
            for j in range(CONV_K - 1):
                off = CONV_HIST - (CONV_K - 1) + j
                acc = acc + buf[off:off + tm, :] * w[j:j + 1]
            y = _silu(acc)
            if sec != SEC_V:
                scale = DN_DK ** -0.5 if sec == SEC_Q else 1.0
                y = y * (lax.rsqrt(jnp.sum(y * y, -1, keepdims=True) + RMS_EPS) * scale)
            qkv_ref[:, cols] = y
        g = c0 // MXU_COLS
        if g < nrest // REST_COLS:
            r0 = g * REST_COLS
            rsec = r0 // d
            w0 = 3 * d + r0 if rsec == SEC_Z else tail0 + r0 - d
            r = _dot_nt(xb, wt_ref[w0:w0 + REST_COLS, :])
            if rsec in (SEC_U, SEC_VG):
                r = _gelu(r)
            rest_ref[:, r0:r0 + REST_COLS] = r.astype(rest_ref.dtype)


def _in_proj(x, w_t, tail0, convw, layer, seq, tm):
    t, d = x.shape
    nrest = 5 * d
    hist_blocks = tm // CONV_HIST
    kern = functools.partial(_in_proj_kernel, tm=tm, d=d, tail0=tail0, tiles_per_seq=seq // tm)
    return pl.pallas_call(
        kern,
        grid=(t // tm,),
        in_specs=[pl.BlockSpec((tm, d), lambda i: (i, 0)),
                  pl.BlockSpec((CONV_HIST, d), lambda i: (jnp.maximum(i * hist_blocks - 1, 0), 0)),
                  _resident(w_t.shape[1:], layer), _resident(convw.shape[1:], layer)],
        out_specs=[pl.BlockSpec((tm, 3 * d), lambda i: (i, 0)),
                   pl.BlockSpec((tm, nrest), lambda i: (i, 0))],
        out_shape=[jax.ShapeDtypeStruct((t, 3 * d), F32),
                   jax.ShapeDtypeStruct((t, nrest), BF16)],
        scratch_shapes=[pltpu.VMEM((CONV_BUFS, CONV_HIST + tm, LANES), F32)],
        compiler_params=pltpu.CompilerParams(
            dimension_semantics=("arbitrary",), vmem_limit_bytes=VMEM_LIMIT),
        name="in_proj",
    )(x, x, w_t, convw)


def _deltanet_kernel(qkv_ref, z_ref, x_ref, wsm_ref, apar_ref, onw_ref,
                     y_ref, s_ref, gb_s, bb_s, u_s, wq_s, at_s, kd_s, el_s, *, nb, ts, d):
    t = pl.program_id(0)
    heads = DN_HEADS
    slot = t % 2
    prev = 1 - slot

    @pl.when(t == 0)
    def _():
        s_ref[...] = jnp.zeros_like(s_ref)
        u_s[1] = jnp.zeros_like(u_s[1])
        wq_s[1] = jnp.zeros_like(wq_s[1])
        at_s[1] = jnp.zeros_like(at_s[1])
        kd_s[1] = jnp.zeros_like(kd_s[1])
        el_s[1] = jnp.zeros_like(el_s[1])

    def q_at(b, rows, cols):
        return qkv_ref[b, rows, SEC_Q * d + cols.start:SEC_Q * d + cols.stop]

    def k_at(b, rows, cols):
        return qkv_ref[b, rows, SEC_K * d + cols.start:SEC_K * d + cols.stop]

    def v_at(b, rows, cols):
        return qkv_ref[b, rows, SEC_V * d + cols.start:SEC_V * d + cols.stop]

    hr = range(heads)
    hs = [slice(h * LANES, (h + 1) * LANES) for h in hr]
    pairs = range(heads // 2)
    nchunk = ts // CHUNK
    onw = onw_ref[...]

    tr = lax.broadcasted_iota(jnp.int32, (ts, ts), 0)
    tc = lax.broadcasted_iota(jnp.int32, (ts, ts), 1)
    tril_bd = ((tr // CHUNK == tc // CHUNK) & (tr >= tc)).astype(BF16)
    for b in range(nb):
        sm = jnp.dot(x_ref[b].astype(BF16), wsm_ref[...], preferred_element_type=F32)
        beta_all = jax.nn.sigmoid(sm)
        g_all = -jnp.exp(apar_ref[0:1, :]) * jax.nn.softplus(sm + apar_ref[1:2, :])
        g_hi = g_all.astype(BF16)
        g_mid = (g_all - g_hi.astype(F32)).astype(BF16)
        g_lo = (g_all - g_hi.astype(F32) - g_mid.astype(F32)).astype(BF16)
        gcum_all = (jnp.dot(tril_bd, g_hi, preferred_element_type=F32)
                    + jnp.dot(tril_bd, g_mid, preferred_element_type=F32)
                    + jnp.dot(tril_bd, g_lo, preferred_element_type=F32))
        for h in hr:
            gb_s[b, :, hs[h]] = jnp.broadcast_to(gcum_all[:, heads + h:heads + h + 1], (ts, LANES))
            bb_s[b, :, hs[h]] = jnp.broadcast_to(beta_all[:, h:h + 1], (ts, LANES))

    prow = lax.broadcasted_iota(jnp.int32, (CHUNK, LANES), 0)
    plane = lax.broadcasted_iota(jnp.int32, (CHUNK, LANES), 1)
    pcol = jnp.where(plane < CHUNK, plane, plane - CHUNK)
    left = plane < CHUNK
    tril_p = prow >= pcol
    strict_p = prow > pcol
    eye_b = prow == pcol
    eye_p = eye_b.astype(F32)
    first_head = lax.broadcasted_iota(jnp.int32, (CHUNK, 2 * LANES), 1) < LANES

    def block_diag(p):
        return jnp.concatenate([jnp.where(left, p, 0.0), jnp.where(left, 0.0, p)], axis=0)

    def prep_chunks():
        units = [(b, c, p) for b in range(nb) for c in range(nchunk) for p in pairs]
        rows = [pl.ds(c * CHUNK, CHUNK) for c in range(nchunk)]
        last = [pl.ds((c + 1) * CHUNK - 1, 1) for c in range(nchunk)]

        a, dec, kb = {}, {}, {}
        for u in units:
            b, c, p = u
            h1, h2 = 2 * p, 2 * p + 1
            cols = slice(h1 * LANES, (h2 + 1) * LANES)
            gi = jnp.where(left, gb_s[b, rows[c], hs[h1]], gb_s[b, rows[c], hs[h2]])
            gj = jnp.sum(jnp.where(eye_b, gi, 0.0), axis=0, keepdims=True)
            dec[u] = jnp.where(tril_p, jnp.exp(jnp.minimum(gi - gj, 0.0)), 0.0)
            k2 = k_at(b, rows[c], cols)
            kb[u] = k2 * bb_s[b, rows[c], cols]
            lhs = jnp.concatenate([kb[u], q_at(b, rows[c], cols)], axis=0)
            rhs = jnp.concatenate([jnp.where(first_head, k2, 0.0),
                                   jnp.where(first_head, 0.0, k2)], axis=0)
            a[u] = _dot_nt(lhs, rhs)
        pw, tinv = {}, {}
        for u in units:
            b, c, p = u
            attn = a[u][CHUNK:] * dec[u]
            at_s[slot, b, c, 2 * p] = attn[:, :CHUNK].astype(BF16)
            at_s[slot, b, c, 2 * p + 1] = attn[:, CHUNK:].astype(BF16)
            pw[u] = -jnp.where(strict_p, a[u][:CHUNK] * dec[u], 0.0)
            tinv[u] = eye_p + pw[u]
        for u in units:
            pw[u] = _dot(pw[u], block_diag(pw[u]))
        for _ in range(4):
            r = {u: _dot(jnp.concatenate([tinv[u], pw[u]], axis=0), block_diag(pw[u])) for u in units}
            for u in units:
                tinv[u] = tinv[u] + r[u][:CHUNK]
                pw[u] = r[u][CHUNK:]
        for u in units:
            tinv[u] = tinv[u] + _dot(tinv[u], block_diag(pw[u]))
        for u in units:
            b, c, p = u
            for j, h in enumerate((2 * p, 2 * p + 1)):
                th = tinv[u][:, j * CHUNK:(j + 1) * CHUNK]
                gi = gb_s[b, rows[c], hs[h]]
                glast = gb_s[b, last[c], hs[h]]
                eg = jnp.exp(gi)
                kbh = kb[u][:, j * LANES:(j + 1) * LANES]
                uw = _dot(th, jnp.concatenate(
                    [v_at(b, rows[c], hs[h]) * bb_s[b, rows[c], hs[h]], kbh * eg], axis=1))
                u_s[slot, b, c, h] = uw[:, :DN_DV]
                wq_s[slot, b, c, h] = jnp.concatenate(
                    [uw[:, DN_DV:], q_at(b, rows[c], hs[h]) * eg], axis=0).astype(BF16)
                kd_s[slot, b, c, h] = (k_at(b, rows[c], hs[h]) * jnp.exp(glast - gi)).astype(BF16)
                el_s[slot, b, c, h:h + 1, :] = jnp.exp(glast)

    def scan_chunk(c):
        rows = pl.ds(c * CHUNK, CHUNK)
        chains = [(b, h) for b in range(nb) for h in hr]
        s_old = {bh: s_ref[bh[0], bh[1]] for bh in chains}
        sw = {(b, h): _dot(wq_s[prev, b, c, h], s_old[b, h]) for (b, h) in chains}
        v_new = {(b, h): u_s[prev, b, c, h] - sw[b, h][:CHUNK] for (b, h) in chains}
        o = {(b, h): sw[b, h][CHUNK:] + _dot(at_s[prev, b, c, h], v_new[b, h]) for (b, h) in chains}
        for (b, h) in chains:
            s_ref[b, h] = (s_old[b, h] * el_s[prev, b, c, h:h + 1, :]
                           + _dot_tn(kd_s[prev, b, c, h], v_new[b, h]))
        for (b, h) in chains:
            z = z_ref[b, rows, hs[h]].astype(F32)
            ob = o[b, h]
            y = ob * lax.rsqrt(jnp.mean(ob * ob, -1, keepdims=True) + RMS_EPS) * onw * _silu(z)
            y_ref[b, rows, hs[h]] = y.astype(y_ref.dtype)

    for c in range(nchunk):
        scan_chunk(c)
    prep_chunks()


def _deltanet(qkv, rest, x, w_small, apar, onw, batch, seq, d, ts):
    nt = seq // ts
    nc = ts // CHUNK
    kern = functools.partial(_deltanet_kernel, nb=batch, ts=ts, d=d)

    def cur(t):
        return jnp.minimum(t, nt - 1)

    def lag(t):
        return jnp.maximum(t - 1, 0)

    y = pl.pallas_call(
        kern,
        grid=(nt + 1,),
        in_specs=[
            pl.BlockSpec((batch, ts, 3 * d), lambda t: (0, cur(t), 0)),
            pl.BlockSpec((batch, ts, d), lambda t: (0, lag(t), SEC_Z)),
            pl.BlockSpec((batch, ts, d), lambda t: (0, cur(t), 0)),
            pl.BlockSpec((d, LANES), lambda t: (0, 0)),
            pl.BlockSpec((2, LANES), lambda t: (0, 0)),
            pl.BlockSpec((1, LANES), lambda t: (0, 0)),
        ],
        out_specs=pl.BlockSpec((batch, ts, d), lambda t: (0, lag(t), 0)),
        out_shape=jax.ShapeDtypeStruct((batch, seq, d), BF16),
        scratch_shapes=[
            pltpu.VMEM((batch, DN_HEADS, DN_DK, DN_DV), F32),
            pltpu.VMEM((batch, ts, d), F32),
            pltpu.VMEM((batch, ts, d), F32),
            pltpu.VMEM((2, batch, nc, DN_HEADS, CHUNK, DN_DV), F32),
            pltpu.VMEM((2, batch, nc, DN_HEADS, 2 * CHUNK, DN_DK), BF16),
            pltpu.VMEM((2, batch, nc, DN_HEADS, CHUNK, CHUNK), BF16),
            pltpu.VMEM((2, batch, nc, DN_HEADS, CHUNK, DN_DK), BF16),
            pltpu.VMEM((2, batch, nc, DN_HEADS, LANES), F32),
        ],
        compiler_params=pltpu.CompilerParams(
            dimension_semantics=("arbitrary",), vmem_limit_bytes=VMEM_LIMIT),
        name="deltanet",
    )(qkv.reshape(batch, seq, 3 * d), rest.reshape(batch, seq, -1), x.reshape(batch, seq, d),
      w_small, apar, onw)
    return y.reshape(batch * seq, d)


def _spatial_gate(u_ref, vg_ref, ws_ref, bst_ref, lng, lnb, tm):
    ri = lax.broadcasted_iota(jnp.int32, (SGU_BLOCK, SGU_BLOCK), 0) // CHUNK
    ci = lax.broadcasted_iota(jnp.int32, (SGU_BLOCK, SGU_BLOCK), 1) // CHUNK
    causal = ri >= ci
    ws = [jnp.where(causal, ws_ref[g], 0.0).astype(BF16) for g in range(SGU_GROUPS)]
    blocks = []
    for n in range(tm // SGU_BLOCK):
        rows = slice(n * SGU_BLOCK, (n + 1) * SGU_BLOCK)
        vn = _layer_norm(vg_ref[rows, :].astype(F32), lng, lnb).astype(BF16)
        parts = []
        for g in range(SGU_GROUPS):
            cs = slice(g * SGU_GROUP_DIM, (g + 1) * SGU_GROUP_DIM)
            sp = jnp.dot(ws[g], vn[:, cs], preferred_element_type=F32) + bst_ref[:, g:g + 1]
            parts.append((u_ref[rows, cs].astype(F32) * sp).astype(BF16))
        blocks.append(jnp.concatenate(parts, axis=1))
    return jnp.concatenate(blocks, axis=0)


def _merge_kernel(ya_ref, u_ref, vg_ref, ga_ref, gb_ref, x_ref, ws_ref, bst_ref, lng_ref, lnb_ref,
                  wpa_ref, wpb_ref, wo_ref, g_ref, b_ref, o_ref, *, alpha, tm):
    yb = _spatial_gate(u_ref, vg_ref, ws_ref, bst_ref, lng_ref[...], lnb_ref[...], tm)
    pa = jnp.dot(ya_ref[...], wpa_ref[...], preferred_element_type=F32)
    pb = jnp.dot(yb, wpb_ref[...], preferred_element_type=F32)
    m = (jax.nn.sigmoid(ga_ref[...].astype(F32)) * pa
         + jax.nn.sigmoid(gb_ref[...].astype(F32)) * pb)
    mix = jnp.dot(m.astype(BF16), wo_ref[...], preferred_element_type=F32)
    o_ref[...] = _layer_norm(alpha * x_ref[...] + mix, g_ref[...], b_ref[...])


def _merge(ya, rest, x, w_s, b_s_t, sgu_g, sgu_b, wpa, wpb, wo, ln_g, ln_b, layer, alpha, tm):
    t_total, d = x.shape
    assert tm % SGU_BLOCK == 0
    kern = functools.partial(_merge_kernel, alpha=alpha, tm=tm)
    row = pl.BlockSpec((tm, d), lambda i: (i, 0))

    def sec(s):
        return pl.BlockSpec((tm, d), lambda i: (i, s))

    return pl.pallas_call(
        kern,
        grid=(t_total // tm,),
        in_specs=[row, sec(SEC_U), sec(SEC_VG), sec(SEC_GA), sec(SEC_GB), row,
                  _resident(w_s.shape[1:], layer), _resident(b_s_t.shape[1:], layer),
                  _resident((1, d)), _resident((1, d)),
                  _resident((d, d), layer), _resident((d, d), layer), _resident((d, d), layer),
                  _resident((1, d)), _resident((1, d))],
        out_specs=row,
        out_shape=jax.ShapeDtypeStruct((t_total, d), F32),
        compiler_params=pltpu.CompilerParams(
            dimension_semantics=("arbitrary",), vmem_limit_bytes=VMEM_LIMIT),
        name="merge",
    )(ya, rest, rest, rest, rest, x, w_s, b_s_t, sgu_g, sgu_b, wpa, wpb, wo, ln_g, ln_b)


def _ffn_kernel(x_ref, wg_ref, wu_ref, wd_ref, g_ref, b_ref, o_ref, *, alpha):
    xb = x_ref[...].astype(BF16)
    gate = jnp.dot(xb, wg_ref[...], preferred_element_type=F32)
    up = jnp.dot(xb, wu_ref[...], preferred_element_type=F32)
    hid = (_silu(gate) * up).astype(BF16)
    ffn = jnp.dot(hid, wd_ref[...], preferred_element_type=F32)
    o_ref[...] = _layer_norm(alpha * x_ref[...] + ffn, g_ref[...], b_ref[...])


def _ffn(x, wg, wu, wd, ln_g, ln_b, layer, alpha, tm):
    t_total, d = x.shape
    hdim = wg.shape[2]
    kern = functools.partial(_ffn_kernel, alpha=alpha)
    row = pl.BlockSpec((tm, d), lambda i: (i, 0))
    return pl.pallas_call(
        kern,
        grid=(t_total // tm,),
        in_specs=[row,
                  _resident((d, hdim), layer), _resident((d, hdim), layer),
                  _resident((hdim, d), layer), _resident((1, d)), _resident((1, d))],
        out_specs=row,
        out_shape=jax.ShapeDtypeStruct((t_total, d), F32),
        compiler_params=pltpu.CompilerParams(
            dimension_semantics=("arbitrary",), vmem_limit_bytes=VMEM_LIMIT),
        name="ffn",
    )(x, wg, wu, wd, ln_g, ln_b)


def _pick_tile(n, want):
    tile = min(n, want)
    assert n % tile == 0, (n, tile)
    return tile


def kernel(x, w_in, conv_w, a_log, dt_bias, o_norm_w, sgu_ln_g, sgu_ln_b, w_s, b_s, w_pa, w_pb, w_o, ln1_g, ln1_b, w_ffn_gate, w_ffn_up, w_ffn_down, ln2_g, ln2_b):
    batch, seq, d = x.shape
    depth = w_in.shape[0]
    assert d == DN_HEADS * DN_DK == SGU_GROUPS * SGU_GROUP_DIM
    assert seq % SGU_BLOCK == 0
    alpha = (2 * depth) ** 0.25
    t_total = batch * seq

    xf = x.reshape(t_total, d)
    nsm = 2 * DN_HEADS
    w_t = jnp.swapaxes(w_in, 1, 2).astype(BF16)
    w_small = jnp.pad(w_in[:, :, 4 * d:4 * d + nsm], ((0, 0), (0, 0), (0, LANES - nsm))).astype(BF16)
    wpa, wpb, wo = w_pa.astype(BF16), w_pb.astype(BF16), w_o.astype(BF16)
    wg, wu, wd = w_ffn_gate.astype(BF16), w_ffn_up.astype(BF16), w_ffn_down.astype(BF16)
    b_s_t = jnp.swapaxes(b_s, 1, 2)
    lane_pad = ((0, 0), (DN_HEADS, LANES - nsm))
    apar = jnp.stack([jnp.pad(a_log, lane_pad), jnp.pad(dt_bias, lane_pad)], axis=1)
    for l in range(depth):
        qkv, rest = _in_proj(xf, w_t, 4 * d + nsm, conv_w, l, seq, _pick_tile(seq, 512))
        ya = _deltanet(qkv, rest, xf, w_small[l], apar[l], o_norm_w[l].reshape(1, DN_DV),
                       batch, seq, d, _pick_tile(seq, 256))
        xf = _merge(ya, rest, xf, w_s, b_s_t, sgu_ln_g[l].reshape(1, d), sgu_ln_b[l].reshape(1, d),
                    wpa, wpb, wo, ln1_g[l].reshape(1, d), ln1_b[l].reshape(1, d),
                    l, alpha, _pick_tile(t_total, 1024))
        xf = _ffn(xf, wg, wu, wd, ln2_g[l].reshape(1, d), ln2_b[l].reshape(1, d),
                  l, alpha, _pick_tile(t_total, 512))
    return xf.reshape(batch, seq, d)
```

```python
import functools
import math

import jax
import jax.numpy as jnp
from jax import lax
from jax.experimental import pallas as pl
from jax.experimental.pallas import tpu as pltpu

F32 = jnp.float32
BF16 = jnp.bfloat16

CHUNK = 64
DN_HEADS = 8
DN_DK = 128
DN_DV = 128
CONV_K = 4
SGU_BLOCK = 128
SGU_GROUPS = 8
SGU_GROUP_DIM = 128
LN_EPS = 1e-5
RMS_EPS = 1e-6
LOG2E = math.log2(math.e)
CONV_HIST = 16
CONV_BUFS = 4

LANES = 128
SUBLANES = 8
MXU_COLS = 256
REST_COLS = 512
VMEM_LIMIT = 56 * 1024 * 1024

SEC_Q, SEC_K, SEC_V = range(3)
SEC_Z, SEC_U, SEC_VG, SEC_GA, SEC_GB = range(5)


def _layer_norm(h, g, b):
    mu = jnp.mean(h, -1, keepdims=True)
    d = h - mu
    var = jnp.mean(d * d, -1, keepdims=True)
    return d * lax.rsqrt(var + LN_EPS) * g + b


def _gelu(x):
    return 0.5 * x * (1.0 + lax.erf(x * (1.0 / math.sqrt(2.0))))


def _silu(x):
    return x / (1.0 + jnp.exp2(x * (-LOG2E)))


def _dot(a, b):
    return jnp.dot(a.astype(BF16), b.astype(BF16), preferred_element_type=F32)


def _dot_nt(a, b):
    return lax.dot_general(a.astype(BF16), b.astype(BF16), (((1,), (1,)), ((), ())),
                           preferred_element_type=F32)


def _dot_tn(a, b):
    return lax.dot_general(a.astype(BF16), b.astype(BF16), (((0,), (0,)), ((), ())),
                           preferred_element_type=F32)


def _resident(shape, layer=None):
    if layer is None:
        return pl.BlockSpec(shape, lambda i: (0,) * len(shape), pipeline_mode=pl.Buffered(1))
    return pl.BlockSpec((None,) + tuple(shape), lambda i: (layer,) + (0,) * len(shape),
                        pipeline_mode=pl.Buffered(1))


def _in_proj_kernel(x_ref, xh_ref, whead_ref, wtail_ref, cw_ref, qkv_ref, rest_ref, stage,
                    *, tm, d, tiles_per_seq):
    i = pl.program_id(0)
    keep_hist = (i % tiles_per_seq != 0).astype(F32)
    xb = x_ref[...].astype(BF16)
    xx = jnp.concatenate([xh_ref[...].astype(BF16), xb], axis=0)
    nrest = rest_ref.shape[1]
    assert d % REST_COLS == 0 and nrest // REST_COLS <= 3 * d // MXU_COLS
    for c0 in range(0, 3 * d, MXU_COLS):
        sec = c0 // d
        p = jnp.dot(xx, whead_ref[:, c0:c0 + MXU_COLS], preferred_element_type=F32)
        for s0 in range(0, MXU_COLS, LANES):
            cols = slice(c0 + s0, c0 + s0 + LANES)
            buf = stage.at[(c0 + s0) // LANES % CONV_BUFS]
            buf[0:CONV_HIST, :] = p[0:CONV_HIST, s0:s0 + LANES] * keep_hist
            buf[CONV_HIST:CONV_HIST + tm, :] = p[CONV_HIST:, s0:s0 + LANES]
            w = cw_ref[:, cols]
            acc = buf[CONV_HIST:CONV_HIST + tm, :] * w[CONV_K - 1:CONV_K]
            for j in range(CONV_K - 1):
                off = CONV_HIST - (CONV_K - 1) + j
                acc = acc + buf[off:off + tm, :] * w[j:j + 1]
            y = _silu(acc)
            if sec != SEC_V:
                scale = DN_DK ** -0.5 if sec == SEC_Q else 1.0
                y = y * (lax.rsqrt(jnp.sum(y * y, -1, keepdims=True) + RMS_EPS) * scale)
            qkv_ref[:, cols] = y
        g = c0 // MXU_COLS
        if g < nrest // REST_COLS:
            r0 = g * REST_COLS
            rsec = r0 // d
            if rsec == SEC_Z:
                w = whead_ref[:, 3 * d + r0:3 * d + r0 + REST_COLS]
            else:
                w = wtail_ref[:, r0 - d:r0 - d + REST_COLS]
            r = jnp.dot(xb, w, preferred_element_type=F32)
            if rsec in (SEC_U, SEC_VG):
                r = _gelu(r)
            rest_ref[:, r0:r0 + REST_COLS] = r.astype(rest_ref.dtype)


def _in_proj(x, w_all, w_tail, convw, layer, seq, tm):
    t, d = x.shape
    nrest = d + w_tail.shape[2]
    hist_blocks = tm // CONV_HIST
    kern = functools.partial(_in_proj_kernel, tm=tm, d=d, tiles_per_seq=seq // tm)
    return pl.pallas_call(
        kern,
        grid=(t // tm,),
        in_specs=[pl.BlockSpec((tm, d), lambda i: (i, 0)),
                  pl.BlockSpec((CONV_HIST, d), lambda i: (jnp.maximum(i * hist_blocks - 1, 0), 0)),
                  _resident((d, 4 * d), layer), _resident(w_tail.shape[1:], layer),
                  _resident(convw.shape[1:], layer)],
        out_specs=[pl.BlockSpec((tm, 3 * d), lambda i: (i, 0)),
                   pl.BlockSpec((tm, nrest), lambda i: (i, 0))],
        out_shape=[jax.ShapeDtypeStruct((t, 3 * d), F32),
                   jax.ShapeDtypeStruct((t, nrest), BF16)],
        scratch_shapes=[pltpu.VMEM((CONV_BUFS, CONV_HIST + tm, LANES), F32)],
        compiler_params=pltpu.CompilerParams(
            dimension_semantics=("arbitrary",), vmem_limit_bytes=VMEM_LIMIT),
        name="in_proj",
    )(x, x, w_all, w_tail, convw)


def _deltanet_kernel(qkv_ref, z_ref, x_ref, wsm_ref, apar_ref, onw_ref,
                     y_ref, s_ref, gb_s, bb_s, u_s, wq_s, at_s, kd_s, el_s, *, nb, ts, d):
    t = pl.program_id(0)
    heads = DN_HEADS

    @pl.when(t == 0)
    def _():
        s_ref[...] = jnp.zeros_like(s_ref)

    def q_at(b, rows, cols):
        return qkv_ref[b, rows, SEC_Q * d + cols.start:SEC_Q * d + cols.stop]

    def k_at(b, rows, cols):
        return qkv_ref[b, rows, SEC_K * d + cols.start:SEC_K * d + cols.stop]

    def v_at(b, rows, cols):
        return qkv_ref[b, rows, SEC_V * d + cols.start:SEC_V * d + cols.stop]

    hr = range(heads)
    hs = [slice(h * LANES, (h + 1) * LANES) for h in hr]
    pairs = range(heads // 2)
    nchunk = ts // CHUNK
    onw = onw_ref[...]

    tr = lax.broadcasted_iota(jnp.int32, (ts, ts), 0)
    tc = lax.broadcasted_iota(jnp.int32, (ts, ts), 1)
    tril_bd = ((tr // CHUNK == tc // CHUNK) & (tr >= tc)).astype(BF16)
    for b in range(nb):
        sm = jnp.dot(x_ref[b].astype(BF16), wsm_ref[...], preferred_element_type=F32)
        beta_all = jax.nn.sigmoid(sm)
        g_all = -jnp.exp(apar_ref[0:1, :]) * jax.nn.softplus(sm + apar_ref[1:2, :])
        g_hi = g_all.astype(BF16)
        g_mid = (g_all - g_hi.astype(F32)).astype(BF16)
        g_lo = (g_all - g_hi.astype(F32) - g_mid.astype(F32)).astype(BF16)
        gcum_all = (jnp.dot(tril_bd, g_hi, preferred_element_type=F32)
                    + jnp.dot(tril_bd, g_mid, preferred_element_type=F32)
                    + jnp.dot(tril_bd, g_lo, preferred_element_type=F32))
        for h in hr:
            gb_s[b, :, hs[h]] = jnp.broadcast_to(gcum_all[:, heads + h:heads + h + 1], (ts, LANES))
            bb_s[b, :, hs[h]] = jnp.broadcast_to(beta_all[:, h:h + 1], (ts, LANES))

    prow = lax.broadcasted_iota(jnp.int32, (CHUNK, LANES), 0)
    plane = lax.broadcasted_iota(jnp.int32, (CHUNK, LANES), 1)
    pcol = jnp.where(plane < CHUNK, plane, plane - CHUNK)
    left = plane < CHUNK
    tril_p = prow >= pcol
    strict_p = prow > pcol
    eye_b = prow == pcol
    eye_p = eye_b.astype(F32)
    first_head = lax.broadcasted_iota(jnp.int32, (CHUNK, 2 * LANES), 1) < LANES

    def block_diag(p):
        return jnp.concatenate([jnp.where(left, p, 0.0), jnp.where(left, 0.0, p)], axis=0)

    def prep_chunks():
        units = [(b, c, p) for b in range(nb) for c in range(nchunk) for p in pairs]
        rows = [pl.ds(c * CHUNK, CHUNK) for c in range(nchunk)]
        last = [pl.ds((c + 1) * CHUNK - 1, 1) for c in range(nchunk)]

        a, dec, kb = {}, {}, {}
        for u in units:
            b, c, p = u
            h1, h2 = 2 * p, 2 * p + 1
            cols = slice(h1 * LANES, (h2 + 1) * LANES)
            gi = jnp.where(left, gb_s[b, rows[c], hs[h1]], gb_s[b, rows[c], hs[h2]])
            gj = jnp.sum(jnp.where(eye_b, gi, 0.0), axis=0, keepdims=True)
            dec[u] = jnp.where(tril_p, jnp.exp(jnp.minimum(gi - gj, 0.0)), 0.0)
            k2 = k_at(b, rows[c], cols)
            kb[u] = k2 * bb_s[b, rows[c], cols]
            lhs = jnp.concatenate([kb[u], q_at(b, rows[c], cols)], axis=0)
            rhs = jnp.concatenate([jnp.where(first_head, k2, 0.0),
                                   jnp.where(first_head, 0.0, k2)], axis=0)
            a[u] = _dot_nt(lhs, rhs)
        pw, tinv = {}, {}
        for u in units:
            b, c, p = u
            attn = a[u][CHUNK:] * dec[u]
            at_s[b, c, 2 * p] = attn[:, :CHUNK].astype(BF16)
            at_s[b, c, 2 * p + 1] = attn[:, CHUNK:].astype(BF16)
            pw[u] = -jnp.where(strict_p, a[u][:CHUNK] * dec[u], 0.0)
            tinv[u] = eye_p + pw[u]
        for u in units:
            pw[u] = _dot(pw[u], block_diag(pw[u]))
        for _ in range(4):
            r = {u: _dot(jnp.concatenate([tinv[u], pw[u]], axis=0), block_diag(pw[u])) for u in units}
            for u in units:
                tinv[u] = tinv[u] + r[u][:CHUNK]
                pw[u] = r[u][CHUNK:]
        for u in units:
            tinv[u] = tinv[u] + _dot(tinv[u], block_diag(pw[u]))
        for u in units:
            b, c, p = u
            for j, h in enumerate((2 * p, 2 * p + 1)):
                th = tinv[u][:, j * CHUNK:(j + 1) * CHUNK]
                gi = gb_s[b, rows[c], hs[h]]
                glast = gb_s[b, last[c], hs[h]]
                eg = jnp.exp(gi)
                kbh = kb[u][:, j * LANES:(j + 1) * LANES]
                uw = _dot(th, jnp.concatenate(
                    [v_at(b, rows[c], hs[h]) * bb_s[b, rows[c], hs[h]], kbh * eg], axis=1))
                u_s[b, c, h] = uw[:, :DN_DV]
                wq_s[b, c, h] = jnp.concatenate(
                    [uw[:, DN_DV:], q_at(b, rows[c], hs[h]) * eg], axis=0).astype(BF16)
                kd_s[b, c, h] = (k_at(b, rows[c], hs[h]) * jnp.exp(glast - gi)).astype(BF16)
                el_s[b, c, h:h + 1, :] = jnp.exp(glast)

    def scan_chunk(c):
        rows = pl.ds(c * CHUNK, CHUNK)
        chains = [(b, h) for b in range(nb) for h in hr]
        s_old = {bh: s_ref[bh[0], bh[1]] for bh in chains}
        sw = {(b, h): _dot(wq_s[b, c, h], s_old[b, h]) for (b, h) in chains}
        v_new = {(b, h): u_s[b, c, h] - sw[b, h][:CHUNK] for (b, h) in chains}
        o = {(b, h): sw[b, h][CHUNK:] + _dot(at_s[b, c, h], v_new[b, h]) for (b, h) in chains}
        for (b, h) in chains:
            s_ref[b, h] = (s_old[b, h] * el_s[b, c, h:h + 1, :]
                           + _dot_tn(kd_s[b, c, h], v_new[b, h]))
        for (b, h) in chains:
            z = z_ref[b, rows, hs[h]].astype(F32)
            ob = o[b, h]
            y = ob * lax.rsqrt(jnp.mean(ob * ob, -1, keepdims=True) + RMS_EPS) * onw * _silu(z)
            y_ref[b, rows, hs[h]] = y.astype(y_ref.dtype)

    prep_chunks()
    for c in range(nchunk):
        scan_chunk(c)


def _deltanet(qkv, rest, x, w_small, apar, onw, batch, seq, d, ts):
    nt = seq // ts
    nc = ts // CHUNK
    kern = functools.partial(_deltanet_kernel, nb=batch, ts=ts, d=d)

    y = pl.pallas_call(
        kern,
        grid=(nt,),
        in_specs=[
            pl.BlockSpec((batch, ts, 3 * d), lambda t: (0, t, 0)),
            pl.BlockSpec((batch, ts, d), lambda t: (0, t, SEC_Z)),
            pl.BlockSpec((batch, ts, d), lambda t: (0, t, 0)),
            pl.BlockSpec((d, LANES), lambda t: (0, 0)),
            pl.BlockSpec((2, LANES), lambda t: (0, 0)),
            pl.BlockSpec((1, LANES), lambda t: (0, 0)),
        ],
        out_specs=pl.BlockSpec((batch, ts, d), lambda t: (0, t, 0)),
        out_shape=jax.ShapeDtypeStruct((batch, seq, d), BF16),
        scratch_shapes=[
            pltpu.VMEM((batch, DN_HEADS, DN_DK, DN_DV), F32),
            pltpu.VMEM((batch, ts, d), F32),
            pltpu.VMEM((batch, ts, d), F32),
            pltpu.VMEM((batch, nc, DN_HEADS, CHUNK, DN_DV), F32),
            pltpu.VMEM((batch, nc, DN_HEADS, 2 * CHUNK, DN_DK), BF16),
            pltpu.VMEM((batch, nc, DN_HEADS, CHUNK, CHUNK), BF16),
            pltpu.VMEM((batch, nc, DN_HEADS, CHUNK, DN_DK), BF16),
            pltpu.VMEM((batch, nc, DN_HEADS, LANES), F32),
        ],
        compiler_params=pltpu.CompilerParams(
            dimension_semantics=("arbitrary",), vmem_limit_bytes=VMEM_LIMIT),
        name="deltanet",
    )(qkv.reshape(batch, seq, 3 * d), rest.reshape(batch, seq, -1), x.reshape(batch, seq, d),
      w_small, apar, onw)
    return y.reshape(batch * seq, d)


def _spatial_gate(u_ref, vg_ref, ws_ref, bst_ref, lng, lnb, tm):
    ri = lax.broadcasted_iota(jnp.int32, (SGU_BLOCK, SGU_BLOCK), 0) // CHUNK
    ci = lax.broadcasted_iota(jnp.int32, (SGU_BLOCK, SGU_BLOCK), 1) // CHUNK
    causal = ri >= ci
    ws = [jnp.where(causal, ws_ref[g], 0.0).astype(BF16) for g in range(SGU_GROUPS)]
    blocks = []
    for n in range(tm // SGU_BLOCK):
        rows = slice(n * SGU_BLOCK, (n + 1) * SGU_BLOCK)
        vn = _layer_norm(vg_ref[rows, :].astype(F32), lng, lnb).astype(BF16)
        parts = []
        for g in range(SGU_GROUPS):
            cs = slice(g * SGU_GROUP_DIM, (g + 1) * SGU_GROUP_DIM)
            sp = jnp.dot(ws[g], vn[:, cs], preferred_element_type=F32) + bst_ref[:, g:g + 1]
            parts.append((u_ref[rows, cs].astype(F32) * sp).astype(BF16))
        blocks.append(jnp.concatenate(parts, axis=1))
    return jnp.concatenate(blocks, axis=0)


def _merge_kernel(ya_ref, u_ref, vg_ref, ga_ref, gb_ref, x_ref, ws_ref, bst_ref, lng_ref, lnb_ref,
                  wpa_ref, wpb_ref, wo_ref, g_ref, b_ref, o_ref, *, alpha, tm):
    yb = _spatial_gate(u_ref, vg_ref, ws_ref, bst_ref, lng_ref[...], lnb_ref[...], tm)
    pa = jnp.dot(ya_ref[...], wpa_ref[...], preferred_element_type=F32)
    pb = jnp.dot(yb, wpb_ref[...], preferred_element_type=F32)
    m = (jax.nn.sigmoid(ga_ref[...].astype(F32)) * pa
         + jax.nn.sigmoid(gb_ref[...].astype(F32)) * pb)
    mix = jnp.dot(m.astype(BF16), wo_ref[...], preferred_element_type=F32)
    o_ref[...] = _layer_norm(alpha * x_ref[...] + mix, g_ref[...], b_ref[...])


def _merge(ya, rest, x, w_s, b_s_t, sgu_g, sgu_b, wpa, wpb, wo, ln_g, ln_b, layer, alpha, tm):
    t_total, d = x.shape
    assert tm % SGU_BLOCK == 0
    kern = functools.partial(_merge_kernel, alpha=alpha, tm=tm)
    row = pl.BlockSpec((tm, d), lambda i: (i, 0))

    def sec(s):
        return pl.BlockSpec((tm, d), lambda i: (i, s))

    return pl.pallas_call(
        kern,
        grid=(t_total // tm,),
        in_specs=[row, sec(SEC_U), sec(SEC_VG), sec(SEC_GA), sec(SEC_GB), row,
                  _resident(w_s.shape[1:], layer), _resident(b_s_t.shape[1:], layer),
                  _resident((1, d)), _resident((1, d)),
                  _resident((d, d), layer), _resident((d, d), layer), _resident((d, d), layer),
                  _resident((1, d)), _resident((1, d))],
        out_specs=row,
        out_shape=jax.ShapeDtypeStruct((t_total, d), F32),
        compiler_params=pltpu.CompilerParams(
            dimension_semantics=("arbitrary",), vmem_limit_bytes=VMEM_LIMIT),
        name="merge",
    )(ya, rest, rest, rest, rest, x, w_s, b_s_t, sgu_g, sgu_b, wpa, wpb, wo, ln_g, ln_b)


def _ffn_kernel(x_ref, wg_ref, wu_ref, wd_ref, g_ref, b_ref, o_ref, *, alpha):
    xb = x_ref[...].astype(BF16)
    gate = jnp.dot(xb, wg_ref[...], preferred_element_type=F32)
    up = jnp.dot(xb, wu_ref[...], preferred_element_type=F32)
    hid = (_silu(gate) * up).astype(BF16)
    ffn = jnp.dot(hid, wd_ref[...], preferred_element_type=F32)
    o_ref[...] = _layer_norm(alpha * x_ref[...] + ffn, g_ref[...], b_ref[...])


def _ffn(x, wg, wu, wd, ln_g, ln_b, layer, alpha, tm):
    t_total, d = x.shape
    hdim = wg.shape[2]
    kern = functools.partial(_ffn_kernel, alpha=alpha)
    row = pl.BlockSpec((tm, d), lambda i: (i, 0))
    return pl.pallas_call(
        kern,
        grid=(t_total // tm,),
        in_specs=[row,
                  _resident((d, hdim), layer), _resident((d, hdim), layer),
                  _resident((hdim, d), layer), _resident((1, d)), _resident((1, d))],
        out_specs=row,
        out_shape=jax.ShapeDtypeStruct((t_total, d), F32),
        compiler_params=pltpu.CompilerParams(
            dimension_semantics=("arbitrary",), vmem_limit_bytes=VMEM_LIMIT),
        name="ffn",
    )(x, wg, wu, wd, ln_g, ln_b)


def _pick_tile(n, want):
    tile = min(n, want)
    assert n % tile == 0, (n, tile)
    return tile


def kernel(x, w_in, conv_w, a_log, dt_bias, o_norm_w, sgu_ln_g, sgu_ln_b, w_s, b_s, w_pa, w_pb, w_o, ln1_g, ln1_b, w_ffn_gate, w_ffn_up, w_ffn_down, ln2_g, ln2_b):
    batch, seq, d = x.shape
    depth = w_in.shape[0]
    assert d == DN_HEADS * DN_DK == SGU_GROUPS * SGU_GROUP_DIM
    assert seq % SGU_BLOCK == 0
    alpha = (2 * depth) ** 0.25
    t_total = batch * seq

    xf = x.reshape(t_total, d)
    nsm = 2 * DN_HEADS
    w_all = w_in.astype(BF16)
    w_tail = w_all[:, :, 4 * d + nsm:]
    w_small = jnp.pad(w_all[:, :, 4 * d:4 * d + nsm], ((0, 0), (0, 0), (0, LANES - nsm)))
    wpa, wpb, wo = w_pa.astype(BF16), w_pb.astype(BF16), w_o.astype(BF16)
    wg, wu, wd = w_ffn_gate.astype(BF16), w_ffn_up.astype(BF16), w_ffn_down.astype(BF16)
    b_s_t = jnp.swapaxes(b_s, 1, 2)
    lane_pad = ((0, 0), (DN_HEADS, LANES - nsm))
    apar = jnp.stack([jnp.pad(a_log, lane_pad), jnp.pad(dt_bias, lane_pad)], axis=1)
    for l in range(depth):
        qkv, rest = _in_proj(xf, w_all, w_tail, conv_w, l, seq, _pick_tile(seq, 512))
        ya = _deltanet(qkv, rest, xf, w_small[l], apar[l], o_norm_w[l].reshape(1, DN_DV),
                       batch, seq, d, _pick_tile(seq, 256))
        xf = _merge(ya, rest, xf, w_s, b_s_t, sgu_ln_g[l].reshape(1, d), sgu_ln_b[l].reshape(1, d),
                    wpa, wpb, wo, ln1_g[l].reshape(1, d), ln1_b[l].reshape(1, d),
                    l, alpha, _pick_tile(t_total, 1024))
        xf = _ffn(xf, wg, wu, wd, ln2_g[l].reshape(1, d), ln2_b[l].reshape(1, d),
                  l, alpha, _pick_tile(t_total, 512))
    return xf.reshape(batch, seq, d)
```

```python
import functools
import math

import jax
import jax.numpy as jnp
from jax import lax
from jax.experimental import pallas as pl
from jax.experimental.pallas import tpu as pltpu

F32 = jnp.float32
BF16 = jnp.bfloat16

CHUNK = 64
DN_HEADS = 8
DN_DK = 128
DN_DV = 128
CONV_K = 4
SGU_BLOCK = 128
SGU_GROUPS = 8
SGU_GROUP_DIM = 128
LN_EPS = 1e-5
RMS_EPS = 1e-6
LOG2E = math.log2(math.e)
CONV_HIST = 16
CONV_BUFS = 4

LANES = 128
SUBLANES = 8
MXU_COLS = 256
REST_COLS = 512
VMEM_LIMIT = 56 * 1024 * 1024

SEC_Q, SEC_K, SEC_V = range(3)
SEC_Z, SEC_U, SEC_VG, SEC_GA, SEC_GB = range(5)


def _layer_norm(h, g, b):
    mu = jnp.mean(h, -1, keepdims=True)
    d = h - mu
    var = jnp.mean(d * d, -1, keepdims=True)
    return d * lax.rsqrt(var + LN_EPS) * g + b


def _gelu(x):
    return 0.5 * x * (1.0 + lax.erf(x * (1.0 / math.sqrt(2.0))))


def _silu(x):
    return x / (1.0 + jnp.exp2(x * (-LOG2E)))


def _dot(a, b):
    return jnp.dot(a.astype(BF16), b.astype(BF16), preferred_element_type=F32)


def _dot_nt(a, b):
    return lax.dot_general(a.astype(BF16), b.astype(BF16), (((1,), (1,)), ((), ())),
                           preferred_element_type=F32)


def _dot_tn(a, b):
    return lax.dot_general(a.astype(BF16), b.astype(BF16), (((0,), (0,)), ((), ())),
                           preferred_element_type=F32)


def _resident(shape, layer=None):
    if layer is None:
        return pl.BlockSpec(shape, lambda i: (0,) * len(shape), pipeline_mode=pl.Buffered(1))
    return pl.BlockSpec((None,) + tuple(shape), lambda i: (layer,) + (0,) * len(shape),
                        pipeline_mode=pl.Buffered(1))


def _in_proj_kernel(x_ref, xh_ref, whead_ref, wtail_ref, cw_ref, qkv_ref, rest_ref, stage,
                    *, tm, d, tiles_per_seq):
    i = pl.program_id(0)
    keep_hist = (i % tiles_per_seq != 0).astype(F32)
    xb = x_ref[...].astype(BF16)
    xx = jnp.concatenate([xh_ref[...].astype(BF16), xb], axis=0)
    nrest = rest_ref.shape[1]
    assert d % REST_COLS == 0 and nrest // REST_COLS <= 3 * d // MXU_COLS
    for c0 in range(0, 3 * d, MXU_COLS):
        sec = c0 // d
        p = jnp.dot(xx, whead_ref[:, c0:c0 + MXU_COLS], preferred_element_type=F32)
        for s0 in range(0, MXU_COLS, LANES):
            cols = slice(c0 + s0, c0 + s0 + LANES)
            buf = stage.at[(c0 + s0) // LANES % CONV_BUFS]
            buf[0:CONV_HIST, :] = p[0:CONV_HIST, s0:s0 + LANES] * keep_hist
            buf[CONV_HIST:CONV_HIST + tm, :] = p[CONV_HIST:, s0:s0 + LANES]
            w = cw_ref[:, cols]
            acc = buf[CONV_HIST:CONV_HIST + tm, :] * w[CONV_K - 1:CONV_K]
            for j in range(CONV_K - 1):
                off = CONV_HIST - (CONV_K - 1) + j
                acc = acc + buf[off:off + tm, :] * w[j:j + 1]
            y = _silu(acc)
            if sec != SEC_V:
                scale = DN_DK ** -0.5 if sec == SEC_Q else 1.0
                y = y * (lax.rsqrt(jnp.sum(y * y, -1, keepdims=True) + RMS_EPS) * scale)
            qkv_ref[:, cols] = y
        g = c0 // MXU_COLS
        if g < nrest // REST_COLS:
            r0 = g * REST_COLS
            rsec = r0 // d
            if rsec == SEC_Z:
                w = whead_ref[:, 3 * d + r0:3 * d + r0 + REST_COLS]
            else:
                w = wtail_ref[:, r0 - d:r0 - d + REST_COLS]
            r = jnp.dot(xb, w, preferred_element_type=F32)
            if rsec in (SEC_U, SEC_VG):
                r = _gelu(r)
            rest_ref[:, r0:r0 + REST_COLS] = r.astype(rest_ref.dtype)


def _in_proj(x, w_all, w_tail, convw, layer, seq, tm):
    t, d = x.shape
    nrest = d + w_tail.shape[2]
    hist_blocks = tm // CONV_HIST
    kern = functools.partial(_in_proj_kernel, tm=tm, d=d, tiles_per_seq=seq // tm)
    return pl.pallas_call(
        kern,
        grid=(t // tm,),
        in_specs=[pl.BlockSpec((tm, d), lambda i: (i, 0)),
                  pl.BlockSpec((CONV_HIST, d), lambda i: (jnp.maximum(i * hist_blocks - 1, 0), 0)),
                  _resident((d, 4 * d), layer), _resident(w_tail.shape[1:], layer),
                  _resident(convw.shape[1:], layer)],
        out_specs=[pl.BlockSpec((tm, 3 * d), lambda i: (i, 0)),
                   pl.BlockSpec((tm, nrest), lambda i: (i, 0))],
        out_shape=[jax.ShapeDtypeStruct((t, 3 * d), F32),
                   jax.ShapeDtypeStruct((t, nrest), BF16)],
        scratch_shapes=[pltpu.VMEM((CONV_BUFS, CONV_HIST + tm, LANES), F32)],
        compiler_params=pltpu.CompilerParams(
            dimension_semantics=("arbitrary",), vmem_limit_bytes=VMEM_LIMIT),
        name="in_proj",
    )(x, x, w_all, w_tail, convw)


def _deltanet_kernel(qkv_ref, z_ref, x_ref, wsm_ref, apar_ref, onw_ref,
                     y_ref, s_ref, gb_s, bb_s, u_s, wq_s, at_s, kd_s, el_s, *, nb, ts, d):
    t = pl.program_id(0)
    heads = DN_HEADS
    slot = t % 2
    prev = 1 - slot

    @pl.when(t == 0)
    def _():
        s_ref[...] = jnp.zeros_like(s_ref)
        u_s[1] = jnp.zeros_like(u_s[1])
        wq_s[1] = jnp.zeros_like(wq_s[1])
        at_s[1] = jnp.zeros_like(at_s[1])
        kd_s[1] = jnp.zeros_like(kd_s[1])
        el_s[1] = jnp.zeros_like(el_s[1])

    def q_at(b, rows, cols):
        return qkv_ref[b, rows, SEC_Q * d + cols.start:SEC_Q * d + cols.stop]

    def k_at(b, rows, cols):
        return qkv_ref[b, rows, SEC_K * d + cols.start:SEC_K * d + cols.stop]

    def v_at(b, rows, cols):
        return qkv_ref[b, rows, SEC_V * d + cols.start:SEC_V * d + cols.stop]

    hr = range(heads)
    hs = [slice(h * LANES, (h + 1) * LANES) for h in hr]
    pairs = range(heads // 2)
    nchunk = ts // CHUNK
    onw = onw_ref[...]

    tr = lax.broadcasted_iota(jnp.int32, (ts, ts), 0)
    tc = lax.broadcasted_iota(jnp.int32, (ts, ts), 1)
    tril_bd = ((tr // CHUNK == tc // CHUNK) & (tr >= tc)).astype(BF16)
    for b in range(nb):
        sm = jnp.dot(x_ref[b].astype(BF16), wsm_ref[...], preferred_element_type=F32)
        beta_all = jax.nn.sigmoid(sm)
        g_all = -jnp.exp(apar_ref[0:1, :]) * jax.nn.softplus(sm + apar_ref[1:2, :])
        g_hi = g_all.astype(BF16)
        g_mid = (g_all - g_hi.astype(F32)).astype(BF16)
        g_lo = (g_all - g_hi.astype(F32) - g_mid.astype(F32)).astype(BF16)
        gcum_all = (jnp.dot(tril_bd, g_hi, preferred_element_type=F32)
                    + jnp.dot(tril_bd, g_mid, preferred_element_type=F32)
                    + jnp.dot(tril_bd, g_lo, preferred_element_type=F32))
        for h in hr:
            gb_s[b, :, hs[h]] = jnp.broadcast_to(gcum_all[:, heads + h:heads + h + 1], (ts, LANES))
            bb_s[b, :, hs[h]] = jnp.broadcast_to(beta_all[:, h:h + 1], (ts, LANES))

    prow = lax.broadcasted_iota(jnp.int32, (CHUNK, LANES), 0)
    plane = lax.broadcasted_iota(jnp.int32, (CHUNK, LANES), 1)
    pcol = jnp.where(plane < CHUNK, plane, plane - CHUNK)
    left = plane < CHUNK
    tril_p = prow >= pcol
    level_mask = {}
    s = 1
    while s < CHUNK:
        level_mask[s] = ((prow // (2 * s) == pcol // (2 * s))
                         & (prow % (2 * s) >= s) & (pcol % (2 * s) < s))
        s *= 2
    eye_b = prow == pcol
    eye_p = eye_b.astype(F32)
    first_head = lax.broadcasted_iota(jnp.int32, (CHUNK, 2 * LANES), 1) < LANES

    def block_diag(p):
        return jnp.concatenate([jnp.where(left, p, 0.0), jnp.where(left, 0.0, p)], axis=0)

    def prep_chunks():
        units = [(b, c, p) for b in range(nb) for c in range(nchunk) for p in pairs]
        rows = [pl.ds(c * CHUNK, CHUNK) for c in range(nchunk)]
        last = [pl.ds((c + 1) * CHUNK - 1, 1) for c in range(nchunk)]

        a, dec, kb = {}, {}, {}
        for u in units:
            b, c, p = u
            h1, h2 = 2 * p, 2 * p + 1
            cols = slice(h1 * LANES, (h2 + 1) * LANES)
            gi = jnp.where(left, gb_s[b, rows[c], hs[h1]], gb_s[b, rows[c], hs[h2]])
            gj = jnp.sum(jnp.where(eye_b, gi, 0.0), axis=0, keepdims=True)
            dec[u] = jnp.where(tril_p, jnp.exp(jnp.minimum(gi - gj, 0.0)), 0.0)
            k2 = k_at(b, rows[c], cols)
            kb[u] = k2 * bb_s[b, rows[c], cols]
            lhs = jnp.concatenate([kb[u], q_at(b, rows[c], cols)], axis=0)
            rhs = jnp.concatenate([jnp.where(first_head, k2, 0.0),
                                   jnp.where(first_head, 0.0, k2)], axis=0)
            a[u] = _dot_nt(lhs, rhs)
        pneg, tinv = {}, {}
        for u in units:
            b, c, p = u
            attn = a[u][CHUNK:] * dec[u]
            at_s[slot, b, c, 2 * p] = attn[:, :CHUNK].astype(BF16)
            at_s[slot, b, c, 2 * p + 1] = attn[:, CHUNK:].astype(BF16)
            pneg[u] = -(a[u][:CHUNK] * dec[u])
            tinv[u] = eye_p + jnp.where(level_mask[1], pneg[u], 0.0)
        s = 2
        while s < CHUNK:
            y = {u: _dot(jnp.where(level_mask[s], pneg[u], 0.0), block_diag(tinv[u])) for u in units}
            for u in units:
                tinv[u] = tinv[u] + _dot(tinv[u], block_diag(y[u]))
            s *= 2
        for u in units:
            b, c, p = u
            for j, h in enumerate((2 * p, 2 * p + 1)):
                th = tinv[u][:, j * CHUNK:(j + 1) * CHUNK]
                gi = gb_s[b, rows[c], hs[h]]
                glast = gb_s[b, last[c], hs[h]]
                eg = jnp.exp(gi)
                kbh = kb[u][:, j * LANES:(j + 1) * LANES]
                uw = _dot(th, jnp.concatenate(
                    [v_at(b, rows[c], hs[h]) * bb_s[b, rows[c], hs[h]], kbh * eg], axis=1))
                u_s[slot, b, c, h] = uw[:, :DN_DV]
                wq_s[slot, b, c, h] = jnp.concatenate(
                    [uw[:, DN_DV:], q_at(b, rows[c], hs[h]) * eg], axis=0).astype(BF16)
                kd_s[slot, b, c, h] = (k_at(b, rows[c], hs[h]) * jnp.exp(glast - gi)).astype(BF16)
                el_s[slot, b, c, h:h + 1, :] = jnp.exp(glast)

    def scan_chunk(c):
        rows = pl.ds(c * CHUNK, CHUNK)
        chains = [(b, h) for b in range(nb) for h in hr]
        s_old = {bh: s_ref[bh[0], bh[1]] for bh in chains}
        sw = {(b, h): _dot(wq_s[prev, b, c, h], s_old[b, h]) for (b, h) in chains}
        v_new = {(b, h): u_s[prev, b, c, h] - sw[b, h][:CHUNK] for (b, h) in chains}
        o = {(b, h): sw[b, h][CHUNK:] + _dot(at_s[prev, b, c, h], v_new[b, h]) for (b, h) in chains}
        for (b, h) in chains:
            s_ref[b, h] = (s_old[b, h] * el_s[prev, b, c, h:h + 1, :]
                           + _dot_tn(kd_s[prev, b, c, h], v_new[b, h]))
        for (b, h) in chains:
            z = z_ref[b, rows, hs[h]].astype(F32)
            ob = o[b, h]
            y = ob * lax.rsqrt(jnp.mean(ob * ob, -1, keepdims=True) + RMS_EPS) * onw * _silu(z)
            y_ref[b, rows, hs[h]] = y.astype(y_ref.dtype)

    for c in range(nchunk):
        scan_chunk(c)
    prep_chunks()


def _deltanet(qkv, rest, x, w_small, apar, onw, batch, seq, d, ts):
    nt = seq // ts
    nc = ts // CHUNK
    kern = functools.partial(_deltanet_kernel, nb=batch, ts=ts, d=d)

    def cur(t):
        return jnp.minimum(t, nt - 1)

    def lag(t):
        return jnp.maximum(t - 1, 0)

    y = pl.pallas_call(
        kern,
        grid=(nt + 1,),
        in_specs=[
            pl.BlockSpec((batch, ts, 3 * d), lambda t: (0, cur(t), 0)),
            pl.BlockSpec((batch, ts, d), lambda t: (0, lag(t), SEC_Z)),
            pl.BlockSpec((batch, ts, d), lambda t: (0, cur(t), 0)),
            pl.BlockSpec((d, LANES), lambda t: (0, 0)),
            pl.BlockSpec((2, LANES), lambda t: (0, 0)),
            pl.BlockSpec((1, LANES), lambda t: (0, 0)),
        ],
        out_specs=pl.BlockSpec((batch, ts, d), lambda t: (0, lag(t), 0)),
        out_shape=jax.ShapeDtypeStruct((batch, seq, d), BF16),
        scratch_shapes=[
            pltpu.VMEM((batch, DN_HEADS, DN_DK, DN_DV), F32),
            pltpu.VMEM((batch, ts, d), F32),
            pltpu.VMEM((batch, ts, d), F32),
            pltpu.VMEM((2, batch, nc, DN_HEADS, CHUNK, DN_DV), F32),
            pltpu.VMEM((2, batch, nc, DN_HEADS, 2 * CHUNK, DN_DK), BF16),
            pltpu.VMEM((2, batch, nc, DN_HEADS, CHUNK, CHUNK), BF16),
            pltpu.VMEM((2, batch, nc, DN_HEADS, CHUNK, DN_DK), BF16),
            pltpu.VMEM((2, batch, nc, DN_HEADS, LANES), F32),
        ],
        compiler_params=pltpu.CompilerParams(
            dimension_semantics=("arbitrary",), vmem_limit_bytes=VMEM_LIMIT),
        name="deltanet",
    )(qkv.reshape(batch, seq, 3 * d), rest.reshape(batch, seq, -1), x.reshape(batch, seq, d),
      w_small, apar, onw)
    return y.reshape(batch * seq, d)


def _spatial_gate(u_ref, vg_ref, ws_ref, bst_ref, lng, lnb, tm):
    ri = lax.broadcasted_iota(jnp.int32, (SGU_BLOCK, SGU_BLOCK), 0) // CHUNK
    ci = lax.broadcasted_iota(jnp.int32, (SGU_BLOCK, SGU_BLOCK), 1) // CHUNK
    causal = ri >= ci
    ws = [jnp.where(causal, ws_ref[g], 0.0).astype(BF16) for g in range(SGU_GROUPS)]
    blocks = []
    for n in range(tm // SGU_BLOCK):
        rows = slice(n * SGU_BLOCK, (n + 1) * SGU_BLOCK)
        vn = _layer_norm(vg_ref[rows, :].astype(F32), lng, lnb).astype(BF16)
        parts = []
        for g in range(SGU_GROUPS):
            cs = slice(g * SGU_GROUP_DIM, (g + 1) * SGU_GROUP_DIM)
            sp = jnp.dot(ws[g], vn[:, cs], preferred_element_type=F32) + bst_ref[:, g:g + 1]
            parts.append((u_ref[rows, cs].astype(F32) * sp).astype(BF16))
        blocks.append(jnp.concatenate(parts, axis=1))
    return jnp.concatenate(blocks, axis=0)


def _merge_kernel(ya_ref, u_ref, vg_ref, ga_ref, gb_ref, x_ref, ws_ref, bst_ref, lng_ref, lnb_ref,
                  wpa_ref, wpb_ref, wo_ref, g_ref, b_ref, o_ref, *, alpha, tm):
    yb = _spatial_gate(u_ref, vg_ref, ws_ref, bst_ref, lng_ref[...], lnb_ref[...], tm)
    pa = jnp.dot(ya_ref[...], wpa_ref[...], preferred_element_type=F32)
    pb = jnp.dot(yb, wpb_ref[...], preferred_element_type=F32)
    m = (jax.nn.sigmoid(ga_ref[...].astype(F32)) * pa
         + jax.nn.sigmoid(gb_ref[...].astype(F32)) * pb)
    mix = jnp.dot(m.astype(BF16), wo_ref[...], preferred_element_type=F32)
    o_ref[...] = _layer_norm(alpha * x_ref[...] + mix, g_ref[...], b_ref[...])


def _merge(ya, rest, x, w_s, b_s_t, sgu_g, sgu_b, wpa, wpb, wo, ln_g, ln_b, layer, alpha, tm):
    t_total, d = x.shape
    assert tm % SGU_BLOCK == 0
    kern = functools.partial(_merge_kernel, alpha=alpha, tm=tm)
    row = pl.BlockSpec((tm, d), lambda i: (i, 0))

    def sec(s):
        return pl.BlockSpec((tm, d), lambda i: (i, s))

    return pl.pallas_call(
        kern,
        grid=(t_total // tm,),
        in_specs=[row, sec(SEC_U), sec(SEC_VG), sec(SEC_GA), sec(SEC_GB), row,
                  _resident(w_s.shape[1:], layer), _resident(b_s_t.shape[1:], layer),
                  _resident((1, d)), _resident((1, d)),
                  _resident((d, d), layer), _resident((d, d), layer), _resident((d, d), layer),
                  _resident((1, d)), _resident((1, d))],
        out_specs=row,
        out_shape=jax.ShapeDtypeStruct((t_total, d), F32),
        compiler_params=pltpu.CompilerParams(
            dimension_semantics=("arbitrary",), vmem_limit_bytes=VMEM_LIMIT),
        name="merge",
    )(ya, rest, rest, rest, rest, x, w_s, b_s_t, sgu_g, sgu_b, wpa, wpb, wo, ln_g, ln_b)


def _ffn_kernel(x_ref, wg_ref, wu_ref, wd_ref, g_ref, b_ref, o_ref, *, alpha):
    xb = x_ref[...].astype(BF16)
    gate = jnp.dot(xb, wg_ref[...], preferred_element_type=F32)
    up = jnp.dot(xb, wu_ref[...], preferred_element_type=F32)
    hid = (_silu(gate) * up).astype(BF16)
    ffn = jnp.dot(hid, wd_ref[...], preferred_element_type=F32)
    o_ref[...] = _layer_norm(alpha * x_ref[...] + ffn, g_ref[...], b_ref[...])


def _ffn(x, wg, wu, wd, ln_g, ln_b, layer, alpha, tm):
    t_total, d = x.shape
    hdim = wg.shape[2]
    kern = functools.partial(_ffn_kernel, alpha=alpha)
    row = pl.BlockSpec((tm, d), lambda i: (i, 0))
    return pl.pallas_call(
        kern,
        grid=(t_total // tm,),
        in_specs=[row,
                  _resident((d, hdim), layer), _resident((d, hdim), layer),
                  _resident((hdim, d), layer), _resident((1, d)), _resident((1, d))],
        out_specs=row,
        out_shape=jax.ShapeDtypeStruct((t_total, d), F32),
        compiler_params=pltpu.CompilerParams(
            dimension_semantics=("arbitrary",), vmem_limit_bytes=VMEM_LIMIT),
        name="ffn",
    )(x, wg, wu, wd, ln_g, ln_b)


def _pick_tile(n, want):
    tile = min(n, want)
    assert n % tile == 0, (n, tile)
    return tile


def kernel(x, w_in, conv_w, a_log, dt_bias, o_norm_w, sgu_ln_g, sgu_ln_b, w_s, b_s, w_pa, w_pb, w_o, ln1_g, ln1_b, w_ffn_gate, w_ffn_up, w_ffn_down, ln2_g, ln2_b):
    batch, seq, d = x.shape
    depth = w_in.shape[0]
    assert d == DN_HEADS * DN_DK == SGU_GROUPS * SGU_GROUP_DIM
    assert seq % SGU_BLOCK == 0
    alpha = (2 * depth) ** 0.25
    t_total = batch * seq

    xf = x.reshape(t_total, d)
    nsm = 2 * DN_HEADS
    w_all = w_in.astype(BF16)
    w_tail = w_all[:, :, 4 * d + nsm:]
    w_small = jnp.pad(w_all[:, :, 4 * d:4 * d + nsm], ((0, 0), (0, 0), (0, LANES - nsm)))
    wpa, wpb, wo = w_pa.astype(BF16), w_pb.astype(BF16), w_o.astype(BF16)
    wg, wu, wd = w_ffn_gate.astype(BF16), w_ffn_up.astype(BF16), w_ffn_down.astype(BF16)
    b_s_t = jnp.swapaxes(b_s, 1, 2)
    lane_pad = ((0, 0), (DN_HEADS, LANES - nsm))
    apar = jnp.stack([jnp.pad(a_log, lane_pad), jnp.pad(dt_bias, lane_pad)], axis=1)
    for l in range(depth):
        qkv, rest = _in_proj(xf, w_all, w_tail, conv_w, l, seq, _pick_tile(seq, 512))
        ya = _deltanet(qkv, rest, xf, w_small[l], apar[l], o_norm_w[l].reshape(1, DN_DV),
                       batch, seq, d, _pick_tile(seq, 256))
        xf = _merge(ya, rest, xf, w_s, b_s_t, sgu_ln_g[l].reshape(1, d), sgu_ln_b[l].reshape(1, d),
                    wpa, wpb, wo, ln1_g[l].reshape(1, d), ln1_b[l].reshape(1, d),
                    l, alpha, _pick_tile(t_total, 1024))
        xf = _ffn(xf, wg, wu, wd, ln2_g[l].reshape(1, d), ln2_b[l].reshape(1, d),
                  l, alpha, _pick_tile(t_total, 512))
    return xf.reshape(batch, seq, d)
```

```python
import functools
import math

import jax
import jax.numpy as jnp
from jax import lax
from jax.experimental import pallas as pl
from jax.experimental.pallas import tpu as pltpu

F32 = jnp.float32
BF16 = jnp.bfloat16

CHUNK = 64
DN_HEADS = 8
DN_DK = 128
DN_DV = 128
CONV_K = 4
SGU_BLOCK = 128
SGU_GROUPS = 8
SGU_GROUP_DIM = 128
LN_EPS = 1e-5
RMS_EPS = 1e-6
LOG2E = math.log2(math.e)
CONV_HIST = 16
CONV_BUFS = 4

LANES = 128
MXU_COLS = 256
REST_COLS = 512
VMEM_LIMIT = 56 * 1024 * 1024
ROWS_IN_PROJ = 512
ROWS_DELTANET = 256
ROWS_MERGE = 1024
ROWS_FFN = 512

SEC_Q, SEC_K, SEC_V = range(3)
SEC_Z, SEC_U, SEC_VG, SEC_GA, SEC_GB = range(5)


def _layer_norm(h, g, b):
    mu = jnp.mean(h, -1, keepdims=True)
    d = h - mu
    var = jnp.mean(d * d, -1, keepdims=True)
    return d * lax.rsqrt(var + LN_EPS) * g + b


def _gelu(x):
    return 0.5 * x * (1.0 + lax.erf(x * (1.0 / math.sqrt(2.0))))


def _silu(x):
    return x / (1.0 + jnp.exp2(x * (-LOG2E)))


def _dot(a, b):
    return jnp.dot(a.astype(BF16), b.astype(BF16), preferred_element_type=F32)


def _dot_nt(a, b):
    return lax.dot_general(a.astype(BF16), b.astype(BF16), (((1,), (1,)), ((), ())),
                           preferred_element_type=F32)


def _dot_tn(a, b):
    return lax.dot_general(a.astype(BF16), b.astype(BF16), (((0,), (0,)), ((), ())),
                           preferred_element_type=F32)


def _resident(shape, layer=None):
    if layer is None:
        return pl.BlockSpec(shape, lambda i: (0,) * len(shape), pipeline_mode=pl.Buffered(1))
    return pl.BlockSpec((None,) + tuple(shape), lambda i: (layer,) + (0,) * len(shape),
                        pipeline_mode=pl.Buffered(1))


def _in_proj_kernel(x_ref, xh_ref, whead_ref, wtail_ref, cw_ref, qkv_ref, rest_ref, stage,
                    *, tm, d, tiles_per_seq):
    i = pl.program_id(0)
    keep_hist = (i % tiles_per_seq != 0).astype(F32)
    xb = x_ref[...].astype(BF16)
    xx = jnp.concatenate([xh_ref[...].astype(BF16), xb], axis=0)
    nrest = rest_ref.shape[1]
    assert d % REST_COLS == 0 and nrest // REST_COLS <= 3 * d // MXU_COLS
    for c0 in range(0, 3 * d, MXU_COLS):
        sec = c0 // d
        p = jnp.dot(xx, whead_ref[:, c0:c0 + MXU_COLS], preferred_element_type=F32)
        for s0 in range(0, MXU_COLS, LANES):
            cols = slice(c0 + s0, c0 + s0 + LANES)
            buf = stage.at[(c0 + s0) // LANES % CONV_BUFS]
            buf[0:CONV_HIST, :] = p[0:CONV_HIST, s0:s0 + LANES] * keep_hist
            buf[CONV_HIST:CONV_HIST + tm, :] = p[CONV_HIST:, s0:s0 + LANES]
            w = cw_ref[:, cols]
            acc = buf[CONV_HIST:CONV_HIST + tm, :] * w[CONV_K - 1:CONV_K]
            for j in range(CONV_K - 1):
                off = CONV_HIST - (CONV_K - 1) + j
                acc = acc + buf[off:off + tm, :] * w[j:j + 1]
            y = _silu(acc)
            if sec != SEC_V:
                scale = DN_DK ** -0.5 if sec == SEC_Q else 1.0
                y = y * (lax.rsqrt(jnp.sum(y * y, -1, keepdims=True) + RMS_EPS) * scale)
            qkv_ref[:, cols] = y
        g = c0 // MXU_COLS
        if g < nrest // REST_COLS:
            r0 = g * REST_COLS
            rsec = r0 // d
            if rsec == SEC_Z:
                w = whead_ref[:, 3 * d + r0:3 * d + r0 + REST_COLS]
            else:
                w = wtail_ref[:, r0 - d:r0 - d + REST_COLS]
            r = jnp.dot(xb, w, preferred_element_type=F32)
            if rsec in (SEC_U, SEC_VG):
                r = _gelu(r)
            rest_ref[:, r0:r0 + REST_COLS] = r.astype(rest_ref.dtype)


def _in_proj(x, w_all, w_tail, convw, layer, seq, tm):
    t, d = x.shape
    nrest = d + w_tail.shape[2]
    hist_blocks = tm // CONV_HIST
    kern = functools.partial(_in_proj_kernel, tm=tm, d=d, tiles_per_seq=seq // tm)
    return pl.pallas_call(
        kern,
        grid=(t // tm,),
        in_specs=[pl.BlockSpec((tm, d), lambda i: (i, 0)),
                  pl.BlockSpec((CONV_HIST, d), lambda i: (jnp.maximum(i * hist_blocks - 1, 0), 0)),
                  _resident((d, 4 * d), layer), _resident(w_tail.shape[1:], layer),
                  _resident(convw.shape[1:], layer)],
        out_specs=[pl.BlockSpec((tm, 3 * d), lambda i: (i, 0)),
                   pl.BlockSpec((tm, nrest), lambda i: (i, 0))],
        out_shape=[jax.ShapeDtypeStruct((t, 3 * d), F32),
                   jax.ShapeDtypeStruct((t, nrest), BF16)],
        scratch_shapes=[pltpu.VMEM((CONV_BUFS, CONV_HIST + tm, LANES), F32)],
        compiler_params=pltpu.CompilerParams(
            dimension_semantics=("arbitrary",), vmem_limit_bytes=VMEM_LIMIT),
        name="in_proj",
    )(x, x, w_all, w_tail, convw)


def _deltanet_kernel(qkv_ref, z_ref, x_ref, wsm_ref, apar_ref, onw_ref,
                     y_ref, s_ref, gb_s, bb_s, u_s, wq_s, at_s, kd_s, el_s, *, nb, ts, d):
    t = pl.program_id(0)
    heads = DN_HEADS
    slot = t % 2
    prev = 1 - slot

    @pl.when(t == 0)
    def _():
        s_ref[...] = jnp.zeros_like(s_ref)
        u_s[1] = jnp.zeros_like(u_s[1])
        wq_s[1] = jnp.zeros_like(wq_s[1])
        at_s[1] = jnp.zeros_like(at_s[1])
        kd_s[1] = jnp.zeros_like(kd_s[1])
        el_s[1] = jnp.zeros_like(el_s[1])

    def q_at(b, rows, cols):
        return qkv_ref[b, rows, SEC_Q * d + cols.start:SEC_Q * d + cols.stop]

    def k_at(b, rows, cols):
        return qkv_ref[b, rows, SEC_K * d + cols.start:SEC_K * d + cols.stop]

    def v_at(b, rows, cols):
        return qkv_ref[b, rows, SEC_V * d + cols.start:SEC_V * d + cols.stop]

    hr = range(heads)
    hs = [slice(h * LANES, (h + 1) * LANES) for h in hr]
    pairs = range(heads // 2)
    nchunk = ts // CHUNK
    onw = onw_ref[...]

    tr = lax.broadcasted_iota(jnp.int32, (ts, ts), 0)
    tc = lax.broadcasted_iota(jnp.int32, (ts, ts), 1)
    tril_bd = ((tr // CHUNK == tc // CHUNK) & (tr >= tc)).astype(BF16)
    for b in range(nb):
        sm = jnp.dot(x_ref[b].astype(BF16), wsm_ref[...], preferred_element_type=F32)
        beta_all = jax.nn.sigmoid(sm)
        g_all = -jnp.exp(apar_ref[0:1, :]) * jax.nn.softplus(sm + apar_ref[1:2, :])
        g_hi = g_all.astype(BF16)
        g_mid = (g_all - g_hi.astype(F32)).astype(BF16)
        g_lo = (g_all - g_hi.astype(F32) - g_mid.astype(F32)).astype(BF16)
        gcum_all = (jnp.dot(tril_bd, g_hi, preferred_element_type=F32)
                    + jnp.dot(tril_bd, g_mid, preferred_element_type=F32)
                    + jnp.dot(tril_bd, g_lo, preferred_element_type=F32))
        for h in hr:
            gb_s[b, :, hs[h]] = jnp.broadcast_to(gcum_all[:, heads + h:heads + h + 1], (ts, LANES))
            bb_s[b, :, hs[h]] = jnp.broadcast_to(beta_all[:, h:h + 1], (ts, LANES))

    prow = lax.broadcasted_iota(jnp.int32, (CHUNK, LANES), 0)
    plane = lax.broadcasted_iota(jnp.int32, (CHUNK, LANES), 1)
    pcol = jnp.where(plane < CHUNK, plane, plane - CHUNK)
    left = plane < CHUNK
    tril_p = prow >= pcol
    level_mask = {}
    s = 1
    while s < CHUNK:
        level_mask[s] = ((prow // (2 * s) == pcol // (2 * s))
                         & (prow % (2 * s) >= s) & (pcol % (2 * s) < s))
        s *= 2
    eye_b = prow == pcol
    eye_p = eye_b.astype(F32)
    first_head = lax.broadcasted_iota(jnp.int32, (CHUNK, 2 * LANES), 1) < LANES

    def block_diag(p):
        return jnp.concatenate([jnp.where(left, p, 0.0), jnp.where(left, 0.0, p)], axis=0)

    def prep_chunks():
        units = [(b, c, p) for b in range(nb) for c in range(nchunk) for p in pairs]
        rows = [pl.ds(c * CHUNK, CHUNK) for c in range(nchunk)]
        last = [pl.ds((c + 1) * CHUNK - 1, 1) for c in range(nchunk)]

        a, dec, kb = {}, {}, {}
        for u in units:
            b, c, p = u
            h1, h2 = 2 * p, 2 * p + 1
            cols = slice(h1 * LANES, (h2 + 1) * LANES)
            gi = jnp.where(left, gb_s[b, rows[c], hs[h1]], gb_s[b, rows[c], hs[h2]])
            gj = jnp.sum(jnp.where(eye_b, gi, 0.0), axis=0, keepdims=True)
            dec[u] = jnp.where(tril_p, jnp.exp(jnp.minimum(gi - gj, 0.0)), 0.0)
            k2 = k_at(b, rows[c], cols)
            kb[u] = k2 * bb_s[b, rows[c], cols]
            lhs = jnp.concatenate([kb[u], q_at(b, rows[c], cols)], axis=0)
            rhs = jnp.concatenate([jnp.where(first_head, k2, 0.0),
                                   jnp.where(first_head, 0.0, k2)], axis=0)
            a[u] = _dot_nt(lhs, rhs)
        pneg, tinv = {}, {}
        for u in units:
            b, c, p = u
            attn = a[u][CHUNK:] * dec[u]
            at_s[slot, b, c, 2 * p] = attn[:, :CHUNK].astype(BF16)
            at_s[slot, b, c, 2 * p + 1] = attn[:, CHUNK:].astype(BF16)
            pneg[u] = -(a[u][:CHUNK] * dec[u])
            tinv[u] = eye_p + jnp.where(level_mask[1], pneg[u], 0.0)
        s = 2
        while s < CHUNK:
            y = {u: _dot(jnp.where(level_mask[s], pneg[u], 0.0), block_diag(tinv[u])) for u in units}
            for u in units:
                tinv[u] = tinv[u] + _dot(tinv[u], block_diag(y[u]))
            s *= 2
        for u in units:
            b, c, p = u
            for j, h in enumerate((2 * p, 2 * p + 1)):
                th = tinv[u][:, j * CHUNK:(j + 1) * CHUNK]
                gi = gb_s[b, rows[c], hs[h]]
                glast = gb_s[b, last[c], hs[h]]
                eg = jnp.exp(gi)
                kbh = kb[u][:, j * LANES:(j + 1) * LANES]
                uw = _dot(th, jnp.concatenate(
                    [v_at(b, rows[c], hs[h]) * bb_s[b, rows[c], hs[h]], kbh * eg], axis=1))
                u_s[slot, b, c, h] = uw[:, :DN_DV]
                wq_s[slot, b, c, h] = jnp.concatenate(
                    [uw[:, DN_DV:], q_at(b, rows[c], hs[h]) * eg], axis=0).astype(BF16)
                kd_s[slot, b, c, h] = (k_at(b, rows[c], hs[h]) * jnp.exp(glast - gi)).astype(BF16)
                el_s[slot, b, c, h:h + 1, :] = jnp.exp(glast)

    def scan_chunk(c):
        rows = pl.ds(c * CHUNK, CHUNK)
        chains = [(b, h) for b in range(nb) for h in hr]
        s_old = {bh: s_ref[bh[0], bh[1]] for bh in chains}
        sw = {(b, h): _dot(wq_s[prev, b, c, h], s_old[b, h]) for (b, h) in chains}
        v_new = {(b, h): u_s[prev, b, c, h] - sw[b, h][:CHUNK] for (b, h) in chains}
        o = {(b, h): sw[b, h][CHUNK:] + _dot(at_s[prev, b, c, h], v_new[b, h]) for (b, h) in chains}
        for (b, h) in chains:
            s_ref[b, h] = (s_old[b, h] * el_s[prev, b, c, h:h + 1, :]
                           + _dot_tn(kd_s[prev, b, c, h], v_new[b, h]))
        for (b, h) in chains:
            z = z_ref[b, rows, hs[h]].astype(F32)
            ob = o[b, h]
            y = ob * lax.rsqrt(jnp.mean(ob * ob, -1, keepdims=True) + RMS_EPS) * onw * _silu(z)
            y_ref[b, rows, hs[h]] = y.astype(y_ref.dtype)

    for c in range(nchunk):
        scan_chunk(c)
    prep_chunks()


def _deltanet(qkv, rest, x, w_small, apar, onw, batch, seq, d, ts):
    nt = seq // ts
    nc = ts // CHUNK
    kern = functools.partial(_deltanet_kernel, nb=batch, ts=ts, d=d)

    def cur(t):
        return jnp.minimum(t, nt - 1)

    def lag(t):
        return jnp.maximum(t - 1, 0)

    y = pl.pallas_call(
        kern,
        grid=(nt + 1,),
        in_specs=[
            pl.BlockSpec((batch, ts, 3 * d), lambda t: (0, cur(t), 0)),
            pl.BlockSpec((batch, ts, d), lambda t: (0, lag(t), SEC_Z)),
            pl.BlockSpec((batch, ts, d), lambda t: (0, cur(t), 0)),
            pl.BlockSpec((d, LANES), lambda t: (0, 0)),
            pl.BlockSpec((2, LANES), lambda t: (0, 0)),
            pl.BlockSpec((1, LANES), lambda t: (0, 0)),
        ],
        out_specs=pl.BlockSpec((batch, ts, d), lambda t: (0, lag(t), 0)),
        out_shape=jax.ShapeDtypeStruct((batch, seq, d), BF16),
        scratch_shapes=[
            pltpu.VMEM((batch, DN_HEADS, DN_DK, DN_DV), F32),
            pltpu.VMEM((batch, ts, d), F32),
            pltpu.VMEM((batch, ts, d), F32),
            pltpu.VMEM((2, batch, nc, DN_HEADS, CHUNK, DN_DV), F32),
            pltpu.VMEM((2, batch, nc, DN_HEADS, 2 * CHUNK, DN_DK), BF16),
            pltpu.VMEM((2, batch, nc, DN_HEADS, CHUNK, CHUNK), BF16),
            pltpu.VMEM((2, batch, nc, DN_HEADS, CHUNK, DN_DK), BF16),
            pltpu.VMEM((2, batch, nc, DN_HEADS, LANES), F32),
        ],
        compiler_params=pltpu.CompilerParams(
            dimension_semantics=("arbitrary",), vmem_limit_bytes=VMEM_LIMIT),
        name="deltanet",
    )(qkv.reshape(batch, seq, 3 * d), rest.reshape(batch, seq, -1), x.reshape(batch, seq, d),
      w_small, apar, onw)
    return y.reshape(batch * seq, d)


def _spatial_gate(u_ref, vg_ref, ws_ref, bst_ref, lng, lnb, tm):
    ri = lax.broadcasted_iota(jnp.int32, (SGU_BLOCK, SGU_BLOCK), 0) // CHUNK
    ci = lax.broadcasted_iota(jnp.int32, (SGU_BLOCK, SGU_BLOCK), 1) // CHUNK
    causal = ri >= ci
    ws = [jnp.where(causal, ws_ref[g], 0.0).astype(BF16) for g in range(SGU_GROUPS)]
    blocks = []
    for n in range(tm // SGU_BLOCK):
        rows = slice(n * SGU_BLOCK, (n + 1) * SGU_BLOCK)
        vn = _layer_norm(vg_ref[rows, :].astype(F32), lng, lnb).astype(BF16)
        parts = []
        for g in range(SGU_GROUPS):
            cs = slice(g * SGU_GROUP_DIM, (g + 1) * SGU_GROUP_DIM)
            sp = jnp.dot(ws[g], vn[:, cs], preferred_element_type=F32) + bst_ref[:, g:g + 1]
            parts.append((u_ref[rows, cs].astype(F32) * sp).astype(BF16))
        blocks.append(jnp.concatenate(parts, axis=1))
    return jnp.concatenate(blocks, axis=0)


def _merge_kernel(ya_ref, u_ref, vg_ref, ga_ref, gb_ref, x_ref, ws_ref, bst_ref, lng_ref, lnb_ref,
                  wpa_ref, wpb_ref, wo_ref, g_ref, b_ref, o_ref, *, alpha, tm):
    yb = _spatial_gate(u_ref, vg_ref, ws_ref, bst_ref, lng_ref[...], lnb_ref[...], tm)
    pa = jnp.dot(ya_ref[...], wpa_ref[...], preferred_element_type=F32)
    pb = jnp.dot(yb, wpb_ref[...], preferred_element_type=F32)
    m = (jax.nn.sigmoid(ga_ref[...].astype(F32)) * pa
         + jax.nn.sigmoid(gb_ref[...].astype(F32)) * pb)
    mix = jnp.dot(m.astype(BF16), wo_ref[...], preferred_element_type=F32)
    o_ref[...] = _layer_norm(alpha * x_ref[...] + mix, g_ref[...], b_ref[...])


def _merge(ya, rest, x, w_s, b_s_t, sgu_g, sgu_b, wpa, wpb, wo, ln_g, ln_b, layer, alpha, tm):
    t_total, d = x.shape
    assert tm % SGU_BLOCK == 0
    kern = functools.partial(_merge_kernel, alpha=alpha, tm=tm)
    row = pl.BlockSpec((tm, d), lambda i: (i, 0))

    def sec(s):
        return pl.BlockSpec((tm, d), lambda i: (i, s))

    return pl.pallas_call(
        kern,
        grid=(t_total // tm,),
        in_specs=[row, sec(SEC_U), sec(SEC_VG), sec(SEC_GA), sec(SEC_GB), row,
                  _resident(w_s.shape[1:], layer), _resident(b_s_t.shape[1:], layer),
                  _resident((1, d)), _resident((1, d)),
                  _resident((d, d), layer), _resident((d, d), layer), _resident((d, d), layer),
                  _resident((1, d)), _resident((1, d))],
        out_specs=row,
        out_shape=jax.ShapeDtypeStruct((t_total, d), F32),
        compiler_params=pltpu.CompilerParams(
            dimension_semantics=("arbitrary",), vmem_limit_bytes=VMEM_LIMIT),
        name="merge",
    )(ya, rest, rest, rest, rest, x, w_s, b_s_t, sgu_g, sgu_b, wpa, wpb, wo, ln_g, ln_b)


def _ffn_kernel(x_ref, wg_ref, wu_ref, wd_ref, g_ref, b_ref, o_ref, *, alpha):
    xb = x_ref[...].astype(BF16)
    gate = jnp.dot(xb, wg_ref[...], preferred_element_type=F32)
    up = jnp.dot(xb, wu_ref[...], preferred_element_type=F32)
    hid = (_silu(gate) * up).astype(BF16)
    ffn = jnp.dot(hid, wd_ref[...], preferred_element_type=F32)
    o_ref[...] = _layer_norm(alpha * x_ref[...] + ffn, g_ref[...], b_ref[...])


def _ffn(x, wg, wu, wd, ln_g, ln_b, layer, alpha, tm):
    t_total, d = x.shape
    hdim = wg.shape[2]
    kern = functools.partial(_ffn_kernel, alpha=alpha)
    row = pl.BlockSpec((tm, d), lambda i: (i, 0))
    return pl.pallas_call(
        kern,
        grid=(t_total // tm,),
        in_specs=[row,
                  _resident((d, hdim), layer), _resident((d, hdim), layer),
                  _resident((hdim, d), layer), _resident((1, d)), _resident((1, d))],
        out_specs=row,
        out_shape=jax.ShapeDtypeStruct((t_total, d), F32),
        compiler_params=pltpu.CompilerParams(
            dimension_semantics=("arbitrary",), vmem_limit_bytes=VMEM_LIMIT),
        name="ffn",
    )(x, wg, wu, wd, ln_g, ln_b)


def _pick_tile(n, want):
    tile = min(n, want)
    assert n % tile == 0, (n, tile)
    return tile


def kernel(x, w_in, conv_w, a_log, dt_bias, o_norm_w, sgu_ln_g, sgu_ln_b, w_s, b_s, w_pa, w_pb, w_o, ln1_g, ln1_b, w_ffn_gate, w_ffn_up, w_ffn_down, ln2_g, ln2_b):
    batch, seq, d = x.shape
    depth = w_in.shape[0]
    assert d == DN_HEADS * DN_DK == SGU_GROUPS * SGU_GROUP_DIM
    assert seq % SGU_BLOCK == 0
    alpha = (2 * depth) ** 0.25
    t_total = batch * seq

    xf = x.reshape(t_total, d)
    nsm = 2 * DN_HEADS
    w_all = w_in.astype(BF16)
    w_tail = w_all[:, :, 4 * d + nsm:]
    w_small = jnp.pad(w_all[:, :, 4 * d:4 * d + nsm], ((0, 0), (0, 0), (0, LANES - nsm)))
    wpa, wpb, wo = w_pa.astype(BF16), w_pb.astype(BF16), w_o.astype(BF16)
    wg, wu, wd = w_ffn_gate.astype(BF16), w_ffn_up.astype(BF16), w_ffn_down.astype(BF16)
    b_s_t = jnp.swapaxes(b_s, 1, 2)
    lane_pad = ((0, 0), (DN_HEADS, LANES - nsm))
    apar = jnp.stack([jnp.pad(a_log, lane_pad), jnp.pad(dt_bias, lane_pad)], axis=1)
    for l in range(depth):
        qkv, rest = _in_proj(xf, w_all, w_tail, conv_w, l, seq, _pick_tile(seq, ROWS_IN_PROJ))
        ya = _deltanet(qkv, rest, xf, w_small[l], apar[l], o_norm_w[l].reshape(1, DN_DV),
                       batch, seq, d, _pick_tile(seq, ROWS_DELTANET))
        xf = _merge(ya, rest, xf, w_s, b_s_t, sgu_ln_g[l].reshape(1, d), sgu_ln_b[l].reshape(1, d),
                    wpa, wpb, wo, ln1_g[l].reshape(1, d), ln1_b[l].reshape(1, d),
                    l, alpha, _pick_tile(t_total, ROWS_MERGE))
        xf = _ffn(xf, wg, wu, wd, ln2_g[l].reshape(1, d), ln2_b[l].reshape(1, d),
                  l, alpha, _pick_tile(t_total, ROWS_FFN))
    return xf.reshape(batch, seq, d)
```

```python
import functools
import math

import jax
import jax.numpy as jnp
from jax import lax
from jax.experimental import pallas as pl
from jax.experimental.pallas import tpu as pltpu

F32 = jnp.float32
BF16 = jnp.bfloat16

CHUNK = 64
DN_HEADS = 8
DN_DK = 128
DN_DV = 128
CONV_K = 4
SGU_BLOCK = 128
SGU_GROUPS = 8
SGU_GROUP_DIM = 128
LN_EPS = 1e-5
RMS_EPS = 1e-6
LOG2E = math.log2(math.e)
CONV_HIST = 16
CONV_BUFS = 4

LANES = 128
MXU_COLS = 256
REST_COLS = 512
VMEM_LIMIT = 56 * 1024 * 1024
ROWS_IN_PROJ = 512
ROWS_DELTANET = 256
ROWS_MERGE = 1024
MERGE_ROWS = 256
ROWS_FFN = 512

SEC_Q, SEC_K, SEC_V = range(3)
SEC_Z, SEC_U, SEC_VG, SEC_GA, SEC_GB = range(5)


def _layer_norm(h, g, b):
    mu = jnp.mean(h, -1, keepdims=True)
    d = h - mu
    var = jnp.mean(d * d, -1, keepdims=True)
    return d * lax.rsqrt(var + LN_EPS) * g + b


def _gelu(x):
    return 0.5 * x * (1.0 + lax.erf(x * (1.0 / math.sqrt(2.0))))


def _silu(x):
    return x / (1.0 + jnp.exp2(x * (-LOG2E)))


def _dot(a, b):
    return jnp.dot(a.astype(BF16), b.astype(BF16), preferred_element_type=F32)


def _dot_nt(a, b):
    return lax.dot_general(a.astype(BF16), b.astype(BF16), (((1,), (1,)), ((), ())),
                           preferred_element_type=F32)


def _dot_tn(a, b):
    return lax.dot_general(a.astype(BF16), b.astype(BF16), (((0,), (0,)), ((), ())),
                           preferred_element_type=F32)


def _resident(shape, layer=None):
    if layer is None:
        return pl.BlockSpec(shape, lambda i: (0,) * len(shape), pipeline_mode=pl.Buffered(1))
    return pl.BlockSpec((None,) + tuple(shape), lambda i: (layer,) + (0,) * len(shape),
                        pipeline_mode=pl.Buffered(1))


def _in_proj_kernel(x_ref, xh_ref, whead_ref, wtail_ref, cw_ref, qkv_ref, rest_ref, stage,
                    *, tm, d, tiles_per_seq):
    i = pl.program_id(0)
    keep_hist = (i % tiles_per_seq != 0).astype(F32)
    xb = x_ref[...].astype(BF16)
    xx = jnp.concatenate([xh_ref[...].astype(BF16), xb], axis=0)
    nrest = rest_ref.shape[1]
    assert d % REST_COLS == 0 and nrest // REST_COLS <= 3 * d // MXU_COLS
    for c0 in range(0, 3 * d, MXU_COLS):
        sec = c0 // d
        p = jnp.dot(xx, whead_ref[:, c0:c0 + MXU_COLS], preferred_element_type=F32)
        for s0 in range(0, MXU_COLS, LANES):
            cols = slice(c0 + s0, c0 + s0 + LANES)
            buf = stage.at[(c0 + s0) // LANES % CONV_BUFS]
            buf[0:CONV_HIST, :] = p[0:CONV_HIST, s0:s0 + LANES] * keep_hist
            buf[CONV_HIST:CONV_HIST + tm, :] = p[CONV_HIST:, s0:s0 + LANES]
            w = cw_ref[:, cols]
            acc = buf[CONV_HIST:CONV_HIST + tm, :] * w[CONV_K - 1:CONV_K]
            for j in range(CONV_K - 1):
                off = CONV_HIST - (CONV_K - 1) + j
                acc = acc + buf[off:off + tm, :] * w[j:j + 1]
            y = _silu(acc)
            if sec != SEC_V:
                scale = DN_DK ** -0.5 if sec == SEC_Q else 1.0
                y = y * (lax.rsqrt(jnp.sum(y * y, -1, keepdims=True) + RMS_EPS) * scale)
            qkv_ref[:, cols] = y
        g = c0 // MXU_COLS
        if g < nrest // REST_COLS:
            r0 = g * REST_COLS
            rsec = r0 // d
            if rsec == SEC_Z:
                w = whead_ref[:, 3 * d + r0:3 * d + r0 + REST_COLS]
            else:
                w = wtail_ref[:, r0 - d:r0 - d + REST_COLS]
            r = jnp.dot(xb, w, preferred_element_type=F32)
            if rsec in (SEC_U, SEC_VG):
                r = _gelu(r)
            rest_ref[:, r0:r0 + REST_COLS] = r.astype(rest_ref.dtype)


def _in_proj(x, w_all, w_tail, convw, layer, seq, tm):
    t, d = x.shape
    nrest = d + w_tail.shape[2]
    hist_blocks = tm // CONV_HIST
    kern = functools.partial(_in_proj_kernel, tm=tm, d=d, tiles_per_seq=seq // tm)
    return pl.pallas_call(
        kern,
        grid=(t // tm,),
        in_specs=[pl.BlockSpec((tm, d), lambda i: (i, 0)),
                  pl.BlockSpec((CONV_HIST, d), lambda i: (jnp.maximum(i * hist_blocks - 1, 0), 0)),
                  _resident((d, 4 * d), layer), _resident(w_tail.shape[1:], layer),
                  _resident(convw.shape[1:], layer)],
        out_specs=[pl.BlockSpec((tm, 3 * d), lambda i: (i, 0)),
                   pl.BlockSpec((tm, nrest), lambda i: (i, 0))],
        out_shape=[jax.ShapeDtypeStruct((t, 3 * d), F32),
                   jax.ShapeDtypeStruct((t, nrest), BF16)],
        scratch_shapes=[pltpu.VMEM((CONV_BUFS, CONV_HIST + tm, LANES), F32)],
        compiler_params=pltpu.CompilerParams(
            dimension_semantics=("arbitrary",), vmem_limit_bytes=VMEM_LIMIT),
        name="in_proj",
    )(x, x, w_all, w_tail, convw)


def _deltanet_kernel(qkv_ref, z_ref, x_ref, wsm_ref, apar_ref, onw_ref,
                     y_ref, s_ref, gb_s, bb_s, u_s, wq_s, at_s, kd_s, el_s, *, nb, ts, d):
    t = pl.program_id(0)
    heads = DN_HEADS
    slot = t % 2
    prev = 1 - slot

    @pl.when(t == 0)
    def _():
        s_ref[...] = jnp.zeros_like(s_ref)
        u_s[1] = jnp.zeros_like(u_s[1])
        wq_s[1] = jnp.zeros_like(wq_s[1])
        at_s[1] = jnp.zeros_like(at_s[1])
        kd_s[1] = jnp.zeros_like(kd_s[1])
        el_s[1] = jnp.zeros_like(el_s[1])

    def q_at(b, rows, cols):
        return qkv_ref[b, rows, SEC_Q * d + cols.start:SEC_Q * d + cols.stop]

    def k_at(b, rows, cols):
        return qkv_ref[b, rows, SEC_K * d + cols.start:SEC_K * d + cols.stop]

    def v_at(b, rows, cols):
        return qkv_ref[b, rows, SEC_V * d + cols.start:SEC_V * d + cols.stop]

    hr = range(heads)
    hs = [slice(h * LANES, (h + 1) * LANES) for h in hr]
    pairs = range(heads // 2)
    nchunk = ts // CHUNK
    onw = onw_ref[...]

    tr = lax.broadcasted_iota(jnp.int32, (ts, ts), 0)
    tc = lax.broadcasted_iota(jnp.int32, (ts, ts), 1)
    tril_bd = ((tr // CHUNK == tc // CHUNK) & (tr >= tc)).astype(BF16)
    for b in range(nb):
        sm = jnp.dot(x_ref[b].astype(BF16), wsm_ref[...], preferred_element_type=F32)
        beta_all = jax.nn.sigmoid(sm)
        g_all = -jnp.exp(apar_ref[0:1, :]) * jax.nn.softplus(sm + apar_ref[1:2, :])
        g_hi = g_all.astype(BF16)
        g_mid = (g_all - g_hi.astype(F32)).astype(BF16)
        g_lo = (g_all - g_hi.astype(F32) - g_mid.astype(F32)).astype(BF16)
        gcum_all = (jnp.dot(tril_bd, g_hi, preferred_element_type=F32)
                    + jnp.dot(tril_bd, g_mid, preferred_element_type=F32)
                    + jnp.dot(tril_bd, g_lo, preferred_element_type=F32))
        for h in hr:
            gb_s[b, :, hs[h]] = jnp.broadcast_to(gcum_all[:, heads + h:heads + h + 1], (ts, LANES))
            bb_s[b, :, hs[h]] = jnp.broadcast_to(beta_all[:, h:h + 1], (ts, LANES))

    prow = lax.broadcasted_iota(jnp.int32, (CHUNK, LANES), 0)
    plane = lax.broadcasted_iota(jnp.int32, (CHUNK, LANES), 1)
    pcol = jnp.where(plane < CHUNK, plane, plane - CHUNK)
    left = plane < CHUNK
    tril_p = prow >= pcol
    level_mask = {}
    s = 1
    while s < CHUNK:
        level_mask[s] = ((prow // (2 * s) == pcol // (2 * s))
                         & (prow % (2 * s) >= s) & (pcol % (2 * s) < s))
        s *= 2
    eye_b = prow == pcol
    eye_p = eye_b.astype(F32)
    first_head = lax.broadcasted_iota(jnp.int32, (CHUNK, 2 * LANES), 1) < LANES

    def block_diag(p):
        return jnp.concatenate([jnp.where(left, p, 0.0), jnp.where(left, 0.0, p)], axis=0)

    def prep_chunks():
        units = [(b, c, p) for b in range(nb) for c in range(nchunk) for p in pairs]
        rows = [pl.ds(c * CHUNK, CHUNK) for c in range(nchunk)]
        last = [pl.ds((c + 1) * CHUNK - 1, 1) for c in range(nchunk)]

        a, dec, kb = {}, {}, {}
        for u in units:
            b, c, p = u
            h1, h2 = 2 * p, 2 * p + 1
            cols = slice(h1 * LANES, (h2 + 1) * LANES)
            gi = jnp.where(left, gb_s[b, rows[c], hs[h1]], gb_s[b, rows[c], hs[h2]])
            gj = jnp.sum(jnp.where(eye_b, gi, 0.0), axis=0, keepdims=True)
            dec[u] = jnp.where(tril_p, jnp.exp(jnp.minimum(gi - gj, 0.0)), 0.0)
            k2 = k_at(b, rows[c], cols)
            kb[u] = k2 * bb_s[b, rows[c], cols]
            lhs = jnp.concatenate([kb[u], q_at(b, rows[c], cols)], axis=0)
            rhs = jnp.concatenate([jnp.where(first_head, k2, 0.0),
                                   jnp.where(first_head, 0.0, k2)], axis=0)
            a[u] = _dot_nt(lhs, rhs)
        pneg, tinv = {}, {}
        for u in units:
            b, c, p = u
            attn = a[u][CHUNK:] * dec[u]
            at_s[slot, b, c, 2 * p] = attn[:, :CHUNK].astype(BF16)
            at_s[slot, b, c, 2 * p + 1] = attn[:, CHUNK:].astype(BF16)
            pneg[u] = -(a[u][:CHUNK] * dec[u])
            tinv[u] = eye_p + jnp.where(level_mask[1], pneg[u], 0.0)
        s = 2
        while s < CHUNK:
            y = {u: _dot(jnp.where(level_mask[s], pneg[u], 0.0), block_diag(tinv[u])) for u in units}
            for u in units:
                tinv[u] = tinv[u] + _dot(tinv[u], block_diag(y[u]))
            s *= 2
        for u in units:
            b, c, p = u
            for j, h in enumerate((2 * p, 2 * p + 1)):
                th = tinv[u][:, j * CHUNK:(j + 1) * CHUNK]
                gi = gb_s[b, rows[c], hs[h]]
                glast = gb_s[b, last[c], hs[h]]
                eg = jnp.exp(gi)
                kbh = kb[u][:, j * LANES:(j + 1) * LANES]
                uw = _dot(th, jnp.concatenate(
                    [v_at(b, rows[c], hs[h]) * bb_s[b, rows[c], hs[h]], kbh * eg], axis=1))
                u_s[slot, b, c, h] = uw[:, :DN_DV]
                wq_s[slot, b, c, h] = jnp.concatenate(
                    [uw[:, DN_DV:], q_at(b, rows[c], hs[h]) * eg], axis=0).astype(BF16)
                kd_s[slot, b, c, h] = (k_at(b, rows[c], hs[h]) * jnp.exp(glast - gi)).astype(BF16)
                el_s[slot, b, c, h:h + 1, :] = jnp.exp(glast)

    def scan_chunk(c):
        rows = pl.ds(c * CHUNK, CHUNK)
        chains = [(b, h) for b in range(nb) for h in hr]
        s_old = {bh: s_ref[bh[0], bh[1]] for bh in chains}
        sw = {(b, h): _dot(wq_s[prev, b, c, h], s_old[b, h]) for (b, h) in chains}
        v_new = {(b, h): u_s[prev, b, c, h] - sw[b, h][:CHUNK] for (b, h) in chains}
        o = {(b, h): sw[b, h][CHUNK:] + _dot(at_s[prev, b, c, h], v_new[b, h]) for (b, h) in chains}
        for (b, h) in chains:
            s_ref[b, h] = (s_old[b, h] * el_s[prev, b, c, h:h + 1, :]
                           + _dot_tn(kd_s[prev, b, c, h], v_new[b, h]))
        for (b, h) in chains:
            z = z_ref[b, rows, hs[h]].astype(F32)
            ob = o[b, h]
            y = ob * lax.rsqrt(jnp.mean(ob * ob, -1, keepdims=True) + RMS_EPS) * onw * _silu(z)
            y_ref[b, rows, hs[h]] = y.astype(y_ref.dtype)

    for c in range(nchunk):
        scan_chunk(c)
    prep_chunks()


def _deltanet(qkv, rest, x, w_small, apar, onw, batch, seq, d, ts):
    nt = seq // ts
    nc = ts // CHUNK
    kern = functools.partial(_deltanet_kernel, nb=batch, ts=ts, d=d)

    def cur(t):
        return jnp.minimum(t, nt - 1)

    def lag(t):
        return jnp.maximum(t - 1, 0)

    y = pl.pallas_call(
        kern,
        grid=(nt + 1,),
        in_specs=[
            pl.BlockSpec((batch, ts, 3 * d), lambda t: (0, cur(t), 0)),
            pl.BlockSpec((batch, ts, d), lambda t: (0, lag(t), SEC_Z)),
            pl.BlockSpec((batch, ts, d), lambda t: (0, cur(t), 0)),
            pl.BlockSpec((d, LANES), lambda t: (0, 0)),
            pl.BlockSpec((2, LANES), lambda t: (0, 0)),
            pl.BlockSpec((1, LANES), lambda t: (0, 0)),
        ],
        out_specs=pl.BlockSpec((batch, ts, d), lambda t: (0, lag(t), 0)),
        out_shape=jax.ShapeDtypeStruct((batch, seq, d), BF16),
        scratch_shapes=[
            pltpu.VMEM((batch, DN_HEADS, DN_DK, DN_DV), F32),
            pltpu.VMEM((batch, ts, d), F32),
            pltpu.VMEM((batch, ts, d), F32),
            pltpu.VMEM((2, batch, nc, DN_HEADS, CHUNK, DN_DV), F32),
            pltpu.VMEM((2, batch, nc, DN_HEADS, 2 * CHUNK, DN_DK), BF16),
            pltpu.VMEM((2, batch, nc, DN_HEADS, CHUNK, CHUNK), BF16),
            pltpu.VMEM((2, batch, nc, DN_HEADS, CHUNK, DN_DK), BF16),
            pltpu.VMEM((2, batch, nc, DN_HEADS, LANES), F32),
        ],
        compiler_params=pltpu.CompilerParams(
            dimension_semantics=("arbitrary",), vmem_limit_bytes=VMEM_LIMIT),
        name="deltanet",
    )(qkv.reshape(batch, seq, 3 * d), rest.reshape(batch, seq, -1), x.reshape(batch, seq, d),
      w_small, apar, onw)
    return y.reshape(batch * seq, d)


def _spatial_gate(u_ref, vg_ref, ws_ref, bst_ref, lng, lnb, tm):
    ri = lax.broadcasted_iota(jnp.int32, (SGU_BLOCK, SGU_BLOCK), 0) // CHUNK
    ci = lax.broadcasted_iota(jnp.int32, (SGU_BLOCK, SGU_BLOCK), 1) // CHUNK
    causal = ri >= ci
    ws = [jnp.where(causal, ws_ref[g], 0.0).astype(BF16) for g in range(SGU_GROUPS)]
    blocks = []
    for n in range(tm // SGU_BLOCK):
        rows = slice(n * SGU_BLOCK, (n + 1) * SGU_BLOCK)
        vn = _layer_norm(vg_ref[rows, :].astype(F32), lng, lnb).astype(BF16)
        parts = []
        for g in range(SGU_GROUPS):
            cs = slice(g * SGU_GROUP_DIM, (g + 1) * SGU_GROUP_DIM)
            sp = jnp.dot(ws[g], vn[:, cs], preferred_element_type=F32) + bst_ref[:, g:g + 1]
            parts.append((u_ref[rows, cs].astype(F32) * sp).astype(BF16))
        blocks.append(jnp.concatenate(parts, axis=1))
    return jnp.concatenate(blocks, axis=0)


def _merge_kernel(ya_ref, u_ref, vg_ref, ga_ref, gb_ref, x_ref, ws_ref, bst_ref, lng_ref, lnb_ref,
                  wpa_ref, wpb_ref, wo_ref, g_ref, b_ref, o_ref, *, alpha, tm):
    sub = MERGE_ROWS
    nsub = tm // sub

    def project(k):
        rows = pl.ds(k * sub, sub)
        yb = _spatial_gate(u_ref.at[rows], vg_ref.at[rows], ws_ref, bst_ref, lng_ref[...], lnb_ref[...], sub)
        pa = jnp.dot(ya_ref[rows, :], wpa_ref[...], preferred_element_type=F32)
        pb = jnp.dot(yb, wpb_ref[...], preferred_element_type=F32)
        return pa, pb

    def finish(k, pa, pb):
        rows = pl.ds(k * sub, sub)
        m = (jax.nn.sigmoid(ga_ref[rows, :].astype(F32)) * pa
             + jax.nn.sigmoid(gb_ref[rows, :].astype(F32)) * pb)
        mix = jnp.dot(m.astype(BF16), wo_ref[...], preferred_element_type=F32)
        o_ref[rows, :] = _layer_norm(alpha * x_ref[rows, :] + mix, g_ref[...], b_ref[...])

    nxt = project(0)
    for k in range(nsub):
        cur = nxt
        if k + 1 < nsub:
            nxt = project(k + 1)
        finish(k, *cur)


def _merge(ya, rest, x, w_s, b_s_t, sgu_g, sgu_b, wpa, wpb, wo, ln_g, ln_b, layer, alpha, tm):
    t_total, d = x.shape
    assert tm % SGU_BLOCK == 0
    kern = functools.partial(_merge_kernel, alpha=alpha, tm=tm)
    row = pl.BlockSpec((tm, d), lambda i: (i, 0))

    def sec(s):
        return pl.BlockSpec((tm, d), lambda i: (i, s))

    return pl.pallas_call(
        kern,
        grid=(t_total // tm,),
        in_specs=[row, sec(SEC_U), sec(SEC_VG), sec(SEC_GA), sec(SEC_GB), row,
                  _resident(w_s.shape[1:], layer), _resident(b_s_t.shape[1:], layer),
                  _resident((1, d)), _resident((1, d)),
                  _resident((d, d), layer), _resident((d, d), layer), _resident((d, d), layer),
                  _resident((1, d)), _resident((1, d))],
        out_specs=row,
        out_shape=jax.ShapeDtypeStruct((t_total, d), F32),
        compiler_params=pltpu.CompilerParams(
            dimension_semantics=("arbitrary",), vmem_limit_bytes=VMEM_LIMIT),
        name="merge",
    )(ya, rest, rest, rest, rest, x, w_s, b_s_t, sgu_g, sgu_b, wpa, wpb, wo, ln_g, ln_b)


def _ffn_kernel(x_ref, wg_ref, wu_ref, wd_ref, g_ref, b_ref, o_ref, *, alpha):
    xb = x_ref[...].astype(BF16)
    gate = jnp.dot(xb, wg_ref[...], preferred_element_type=F32)
    up = jnp.dot(xb, wu_ref[...], preferred_element_type=F32)
    hid = (_silu(gate) * up).astype(BF16)
    ffn = jnp.dot(hid, wd_ref[...], preferred_element_type=F32)
    o_ref[...] = _layer_norm(alpha * x_ref[...] + ffn, g_ref[...], b_ref[...])


def _ffn(x, wg, wu, wd, ln_g, ln_b, layer, alpha, tm):
    t_total, d = x.shape
    hdim = wg.shape[2]
    kern = functools.partial(_ffn_kernel, alpha=alpha)
    row = pl.BlockSpec((tm, d), lambda i: (i, 0))
    return pl.pallas_call(
        kern,
        grid=(t_total // tm,),
        in_specs=[row,
                  _resident((d, hdim), layer), _resident((d, hdim), layer),
                  _resident((hdim, d), layer), _resident((1, d)), _resident((1, d))],
        out_specs=row,
        out_shape=jax.ShapeDtypeStruct((t_total, d), F32),
        compiler_params=pltpu.CompilerParams(
            dimension_semantics=("arbitrary",), vmem_limit_bytes=VMEM_LIMIT),
        name="ffn",
    )(x, wg, wu, wd, ln_g, ln_b)


def _pick_tile(n, want):
    tile = min(n, want)
    assert n % tile == 0, (n, tile)
    return tile


def kernel(x, w_in, conv_w, a_log, dt_bias, o_norm_w, sgu_ln_g, sgu_ln_b, w_s, b_s, w_pa, w_pb, w_o, ln1_g, ln1_b, w_ffn_gate, w_ffn_up, w_ffn_down, ln2_g, ln2_b):
    batch, seq, d = x.shape
    depth = w_in.shape[0]
    assert d == DN_HEADS * DN_DK == SGU_GROUPS * SGU_GROUP_DIM
    assert seq % SGU_BLOCK == 0
    alpha = (2 * depth) ** 0.25
    t_total = batch * seq

    xf = x.reshape(t_total, d)
    nsm = 2 * DN_HEADS
    w_all = w_in.astype(BF16)
    w_tail = w_all[:, :, 4 * d + nsm:]
    w_small = jnp.pad(w_all[:, :, 4 * d:4 * d + nsm], ((0, 0), (0, 0), (0, LANES - nsm)))
    wpa, wpb, wo = w_pa.astype(BF16), w_pb.astype(BF16), w_o.astype(BF16)
    wg, wu, wd = w_ffn_gate.astype(BF16), w_ffn_up.astype(BF16), w_ffn_down.astype(BF16)
    b_s_t = jnp.swapaxes(b_s, 1, 2)
    lane_pad = ((0, 0), (DN_HEADS, LANES - nsm))
    apar = jnp.stack([jnp.pad(a_log, lane_pad), jnp.pad(dt_bias, lane_pad)], axis=1)
    for l in range(depth):
        qkv, rest = _in_proj(xf, w_all, w_tail, conv_w, l, seq, _pick_tile(seq, ROWS_IN_PROJ))
        ya = _deltanet(qkv, rest, xf, w_small[l], apar[l], o_norm_w[l].reshape(1, DN_DV),
                       batch, seq, d, _pick_tile(seq, ROWS_DELTANET))
        xf = _merge(ya, rest, xf, w_s, b_s_t, sgu_ln_g[l].reshape(1, d), sgu_ln_b[l].reshape(1, d),
                    wpa, wpb, wo, ln1_g[l].reshape(1, d), ln1_b[l].reshape(1, d),
                    l, alpha, _pick_tile(t_total, ROWS_MERGE))
        xf = _ffn(xf, wg, wu, wd, ln2_g[l].reshape(1, d), ln2_b[l].reshape(1, d),
                  l, alpha, _pick_tile(t_total, ROWS_FFN))
    return xf.reshape(batch, seq, d)
```

```python
import functools
import math

import jax
import jax.numpy as jnp
from jax import lax
from jax.experimental import pallas as pl
from jax.experimental.pallas import tpu as pltpu

F32 = jnp.float32
BF16 = jnp.bfloat16

CHUNK = 64
DN_HEADS = 8
DN_DK = 128
DN_DV = 128
CONV_K = 4
SGU_BLOCK = 128
SGU_GROUPS = 8
SGU_GROUP_DIM = 128
LN_EPS = 1e-5
RMS_EPS = 1e-6
LOG2E = math.log2(math.e)
CONV_HIST = 16
CONV_BUFS = 4

LANES = 128
MXU_COLS = 256
REST_COLS = 512
VMEM_LIMIT = 56 * 1024 * 1024
ROWS_IN_PROJ = 512
ROWS_DELTANET = 256
ROWS_MERGE = 1024
ROWS_FFN = 512

SEC_Q, SEC_K, SEC_V = range(3)
SEC_Z, SEC_U, SEC_VG, SEC_GA, SEC_GB = range(5)


def _layer_norm(h, g, b):
    mu = jnp.mean(h, -1, keepdims=True)
    d = h - mu
    var = jnp.mean(d * d, -1, keepdims=True)
    return d * lax.rsqrt(var + LN_EPS) * g + b


def _gelu(x):
    return 0.5 * x * (1.0 + lax.erf(x * (1.0 / math.sqrt(2.0))))


def _silu(x):
    return x / (1.0 + jnp.exp2(x * (-LOG2E)))


def _dot(a, b):
    return jnp.dot(a.astype(BF16), b.astype(BF16), preferred_element_type=F32)


def _dot_nt(a, b):
    return lax.dot_general(a.astype(BF16), b.astype(BF16), (((1,), (1,)), ((), ())),
                           preferred_element_type=F32)


def _dot_tn(a, b):
    return lax.dot_general(a.astype(BF16), b.astype(BF16), (((0,), (0,)), ((), ())),
                           preferred_element_type=F32)


def _resident(shape, layer=None):
    if layer is None:
        return pl.BlockSpec(shape, lambda i: (0,) * len(shape), pipeline_mode=pl.Buffered(1))
    return pl.BlockSpec((None,) + tuple(shape), lambda i: (layer,) + (0,) * len(shape),
                        pipeline_mode=pl.Buffered(1))


def _in_proj_kernel(x_ref, xh_ref, whead_ref, wtail_ref, cw_ref, qkv_ref, z_ref, u_ref, vg_ref, ga_ref,
                    gb_ref, stage, *, tm, d, tiles_per_seq):
    i = pl.program_id(0)
    keep_hist = (i % tiles_per_seq != 0).astype(F32)
    xb = x_ref[...].astype(BF16)
    xx = jnp.concatenate([xh_ref[...].astype(BF16), xb], axis=0)
    rest_refs = (z_ref, u_ref, vg_ref, ga_ref, gb_ref)
    nrest = len(rest_refs) * d
    assert d % REST_COLS == 0 and nrest // REST_COLS <= 3 * d // MXU_COLS
    for c0 in range(0, 3 * d, MXU_COLS):
        sec = c0 // d
        p = jnp.dot(xx, whead_ref[:, c0:c0 + MXU_COLS], preferred_element_type=F32)
        for s0 in range(0, MXU_COLS, LANES):
            cols = slice(c0 + s0, c0 + s0 + LANES)
            buf = stage.at[(c0 + s0) // LANES % CONV_BUFS]
            buf[0:CONV_HIST, :] = p[0:CONV_HIST, s0:s0 + LANES] * keep_hist
            buf[CONV_HIST:CONV_HIST + tm, :] = p[CONV_HIST:, s0:s0 + LANES]
            w = cw_ref[:, cols]
            acc = buf[CONV_HIST:CONV_HIST + tm, :] * w[CONV_K - 1:CONV_K]
            for j in range(CONV_K - 1):
                off = CONV_HIST - (CONV_K - 1) + j
                acc = acc + buf[off:off + tm, :] * w[j:j + 1]
            y = _silu(acc)
            if sec != SEC_V:
                scale = DN_DK ** -0.5 if sec == SEC_Q else 1.0
                y = y * (lax.rsqrt(jnp.sum(y * y, -1, keepdims=True) + RMS_EPS) * scale)
            qkv_ref[:, cols] = y
        g = c0 // MXU_COLS
        if g < nrest // REST_COLS:
            r0 = g * REST_COLS
            rsec = r0 // d
            if rsec == SEC_Z:
                w = whead_ref[:, 3 * d + r0:3 * d + r0 + REST_COLS]
            else:
                w = wtail_ref[:, r0 - d:r0 - d + REST_COLS]
            r = jnp.dot(xb, w, preferred_element_type=F32)
            if rsec in (SEC_U, SEC_VG):
                r = _gelu(r)
            rest_refs[rsec][:, r0 - rsec * d:r0 - rsec * d + REST_COLS] = r.astype(BF16)


def _in_proj(x, w_all, w_tail, convw, layer, seq, tm):
    t, d = x.shape
    nrest = d + w_tail.shape[2]
    hist_blocks = tm // CONV_HIST
    kern = functools.partial(_in_proj_kernel, tm=tm, d=d, tiles_per_seq=seq // tm)
    return pl.pallas_call(
        kern,
        grid=(t // tm,),
        in_specs=[pl.BlockSpec((tm, d), lambda i: (i, 0)),
                  pl.BlockSpec((CONV_HIST, d), lambda i: (jnp.maximum(i * hist_blocks - 1, 0), 0)),
                  _resident((d, 4 * d), layer), _resident(w_tail.shape[1:], layer),
                  _resident(convw.shape[1:], layer)],
        out_specs=[pl.BlockSpec((tm, 3 * d), lambda i: (i, 0))]
        + [pl.BlockSpec((tm, d), lambda i: (i, 0))] * (nrest // d),
        out_shape=[jax.ShapeDtypeStruct((t, 3 * d), F32)]
        + [jax.ShapeDtypeStruct((t, d), BF16)] * (nrest // d),
        scratch_shapes=[pltpu.VMEM((CONV_BUFS, CONV_HIST + tm, LANES), F32)],
        compiler_params=pltpu.CompilerParams(
            dimension_semantics=("arbitrary",), vmem_limit_bytes=VMEM_LIMIT),
        name="in_proj",
    )(x, x, w_all, w_tail, convw)


def _deltanet_kernel(qkv_ref, z_ref, x_ref, wsm_ref, apar_ref, onw_ref,
                     y_ref, s_ref, gb_s, bb_s, u_s, wq_s, at_s, kd_s, el_s, *, nb, ts, d):
    t = pl.program_id(0)
    heads = DN_HEADS
    slot = t % 2
    prev = 1 - slot

    @pl.when(t == 0)
    def _():
        s_ref[...] = jnp.zeros_like(s_ref)
        u_s[1] = jnp.zeros_like(u_s[1])
        wq_s[1] = jnp.zeros_like(wq_s[1])
        at_s[1] = jnp.zeros_like(at_s[1])
        kd_s[1] = jnp.zeros_like(kd_s[1])
        el_s[1] = jnp.zeros_like(el_s[1])

    def q_at(b, rows, cols):
        return qkv_ref[b, rows, SEC_Q * d + cols.start:SEC_Q * d + cols.stop]

    def k_at(b, rows, cols):
        return qkv_ref[b, rows, SEC_K * d + cols.start:SEC_K * d + cols.stop]

    def v_at(b, rows, cols):
        return qkv_ref[b, rows, SEC_V * d + cols.start:SEC_V * d + cols.stop]

    hr = range(heads)
    hs = [slice(h * LANES, (h + 1) * LANES) for h in hr]
    pairs = range(heads // 2)
    nchunk = ts // CHUNK
    onw = onw_ref[...]

    tr = lax.broadcasted_iota(jnp.int32, (ts, ts), 0)
    tc = lax.broadcasted_iota(jnp.int32, (ts, ts), 1)
    tril_bd = ((tr // CHUNK == tc // CHUNK) & (tr >= tc)).astype(BF16)
    for b in range(nb):
        sm = jnp.dot(x_ref[b].astype(BF16), wsm_ref[...], preferred_element_type=F32)
        beta_all = jax.nn.sigmoid(sm)
        g_all = -jnp.exp(apar_ref[0:1, :]) * jax.nn.softplus(sm + apar_ref[1:2, :])
        g_hi = g_all.astype(BF16)
        g_mid = (g_all - g_hi.astype(F32)).astype(BF16)
        g_lo = (g_all - g_hi.astype(F32) - g_mid.astype(F32)).astype(BF16)
        gcum_all = (jnp.dot(tril_bd, g_hi, preferred_element_type=F32)
                    + jnp.dot(tril_bd, g_mid, preferred_element_type=F32)
                    + jnp.dot(tril_bd, g_lo, preferred_element_type=F32))
        for h in hr:
            gb_s[b, :, hs[h]] = jnp.broadcast_to(gcum_all[:, heads + h:heads + h + 1], (ts, LANES))
            bb_s[b, :, hs[h]] = jnp.broadcast_to(beta_all[:, h:h + 1], (ts, LANES))

    prow = lax.broadcasted_iota(jnp.int32, (CHUNK, LANES), 0)
    plane = lax.broadcasted_iota(jnp.int32, (CHUNK, LANES), 1)
    pcol = jnp.where(plane < CHUNK, plane, plane - CHUNK)
    left = plane < CHUNK
    tril_p = prow >= pcol
    level_mask = {}
    s = 1
    while s < CHUNK:
        level_mask[s] = ((prow // (2 * s) == pcol // (2 * s))
                         & (prow % (2 * s) >= s) & (pcol % (2 * s) < s))
        s *= 2
    eye_b = prow == pcol
    eye_p = eye_b.astype(F32)
    first_head = lax.broadcasted_iota(jnp.int32, (CHUNK, 2 * LANES), 1) < LANES

    def block_diag(p):
        return jnp.concatenate([jnp.where(left, p, 0.0), jnp.where(left, 0.0, p)], axis=0)

    def prep_chunks():
        units = [(b, c, p) for b in range(nb) for c in range(nchunk) for p in pairs]
        rows = [pl.ds(c * CHUNK, CHUNK) for c in range(nchunk)]
        last = [pl.ds((c + 1) * CHUNK - 1, 1) for c in range(nchunk)]

        a, dec, kb = {}, {}, {}
        for u in units:
            b, c, p = u
            h1, h2 = 2 * p, 2 * p + 1
            cols = slice(h1 * LANES, (h2 + 1) * LANES)
            gi = jnp.where(left, gb_s[b, rows[c], hs[h1]], gb_s[b, rows[c], hs[h2]])
            gj = jnp.sum(jnp.where(eye_b, gi, 0.0), axis=0, keepdims=True)
            dec[u] = jnp.where(tril_p, jnp.exp(jnp.minimum(gi - gj, 0.0)), 0.0)
            k2 = k_at(b, rows[c], cols)
            kb[u] = k2 * bb_s[b, rows[c], cols]
            lhs = jnp.concatenate([kb[u], q_at(b, rows[c], cols)], axis=0)
            rhs = jnp.concatenate([jnp.where(first_head, k2, 0.0),
                                   jnp.where(first_head, 0.0, k2)], axis=0)
            a[u] = _dot_nt(lhs, rhs)
        pneg, tinv = {}, {}
        for u in units:
            b, c, p = u
            attn = a[u][CHUNK:] * dec[u]
            at_s[slot, b, c, 2 * p] = attn[:, :CHUNK].astype(BF16)
            at_s[slot, b, c, 2 * p + 1] = attn[:, CHUNK:].astype(BF16)
            pneg[u] = -(a[u][:CHUNK] * dec[u])
            tinv[u] = eye_p + jnp.where(level_mask[1], pneg[u], 0.0)
        s = 2
        while s < CHUNK:
            y = {u: _dot(jnp.where(level_mask[s], pneg[u], 0.0), block_diag(tinv[u])) for u in units}
            for u in units:
                tinv[u] = tinv[u] + _dot(tinv[u], block_diag(y[u]))
            s *= 2
        for u in units:
            b, c, p = u
            for j, h in enumerate((2 * p, 2 * p + 1)):
                th = tinv[u][:, j * CHUNK:(j + 1) * CHUNK]
                gi = gb_s[b, rows[c], hs[h]]
                glast = gb_s[b, last[c], hs[h]]
                eg = jnp.exp(gi)
                kbh = kb[u][:, j * LANES:(j + 1) * LANES]
                uw = _dot(th, jnp.concatenate(
                    [v_at(b, rows[c], hs[h]) * bb_s[b, rows[c], hs[h]], kbh * eg], axis=1))
                u_s[slot, b, c, h] = uw[:, :DN_DV]
                wq_s[slot, b, c, h] = jnp.concatenate(
                    [uw[:, DN_DV:], q_at(b, rows[c], hs[h]) * eg], axis=0).astype(BF16)
                kd_s[slot, b, c, h] = (k_at(b, rows[c], hs[h]) * jnp.exp(glast - gi)).astype(BF16)
                el_s[slot, b, c, h:h + 1, :] = jnp.exp(glast)

    def scan_chunk(c):
        rows = pl.ds(c * CHUNK, CHUNK)
        chains = [(b, h) for b in range(nb) for h in hr]
        s_old = {bh: s_ref[bh[0], bh[1]] for bh in chains}
        sw = {(b, h): _dot(wq_s[prev, b, c, h], s_old[b, h]) for (b, h) in chains}
        v_new = {(b, h): u_s[prev, b, c, h] - sw[b, h][:CHUNK] for (b, h) in chains}
        o = {(b, h): sw[b, h][CHUNK:] + _dot(at_s[prev, b, c, h], v_new[b, h]) for (b, h) in chains}
        for (b, h) in chains:
            s_ref[b, h] = (s_old[b, h] * el_s[prev, b, c, h:h + 1, :]
                           + _dot_tn(kd_s[prev, b, c, h], v_new[b, h]))
        for (b, h) in chains:
            z = z_ref[b, rows, hs[h]].astype(F32)
            ob = o[b, h]
            y = ob * lax.rsqrt(jnp.mean(ob * ob, -1, keepdims=True) + RMS_EPS) * onw * _silu(z)
            y_ref[b, rows, hs[h]] = y.astype(y_ref.dtype)

    for c in range(nchunk):
        scan_chunk(c)
    prep_chunks()


def _deltanet(qkv, z, x, w_small, apar, onw, batch, seq, d, ts):
    nt = seq // ts
    nc = ts // CHUNK
    kern = functools.partial(_deltanet_kernel, nb=batch, ts=ts, d=d)

    def cur(t):
        return jnp.minimum(t, nt - 1)

    def lag(t):
        return jnp.maximum(t - 1, 0)

    y = pl.pallas_call(
        kern,
        grid=(nt + 1,),
        in_specs=[
            pl.BlockSpec((batch, ts, 3 * d), lambda t: (0, cur(t), 0)),
            pl.BlockSpec((batch, ts, d), lambda t: (0, lag(t), 0)),
            pl.BlockSpec((batch, ts, d), lambda t: (0, cur(t), 0)),
            pl.BlockSpec((d, LANES), lambda t: (0, 0)),
            pl.BlockSpec((2, LANES), lambda t: (0, 0)),
            pl.BlockSpec((1, LANES), lambda t: (0, 0)),
        ],
        out_specs=pl.BlockSpec((batch, ts, d), lambda t: (0, lag(t), 0)),
        out_shape=jax.ShapeDtypeStruct((batch, seq, d), BF16),
        scratch_shapes=[
            pltpu.VMEM((batch, DN_HEADS, DN_DK, DN_DV), F32),
            pltpu.VMEM((batch, ts, d), F32),
            pltpu.VMEM((batch, ts, d), F32),
            pltpu.VMEM((2, batch, nc, DN_HEADS, CHUNK, DN_DV), F32),
            pltpu.VMEM((2, batch, nc, DN_HEADS, 2 * CHUNK, DN_DK), BF16),
            pltpu.VMEM((2, batch, nc, DN_HEADS, CHUNK, CHUNK), BF16),
            pltpu.VMEM((2, batch, nc, DN_HEADS, CHUNK, DN_DK), BF16),
            pltpu.VMEM((2, batch, nc, DN_HEADS, LANES), F32),
        ],
        compiler_params=pltpu.CompilerParams(
            dimension_semantics=("arbitrary",), vmem_limit_bytes=VMEM_LIMIT),
        name="deltanet",
    )(qkv.reshape(batch, seq, 3 * d), z.reshape(batch, seq, d), x.reshape(batch, seq, d),
      w_small, apar, onw)
    return y.reshape(batch * seq, d)


def _spatial_gate(u_ref, vg_ref, ws_ref, bst_ref, lng, lnb, tm):
    ri = lax.broadcasted_iota(jnp.int32, (SGU_BLOCK, SGU_BLOCK), 0) // CHUNK
    ci = lax.broadcasted_iota(jnp.int32, (SGU_BLOCK, SGU_BLOCK), 1) // CHUNK
    causal = ri >= ci
    ws = [jnp.where(causal, ws_ref[g], 0.0).astype(BF16) for g in range(SGU_GROUPS)]
    blocks = []
    for n in range(tm // SGU_BLOCK):
        rows = slice(n * SGU_BLOCK, (n + 1) * SGU_BLOCK)
        vn = _layer_norm(vg_ref[rows, :].astype(F32), lng, lnb).astype(BF16)
        parts = []
        for g in range(SGU_GROUPS):
            cs = slice(g * SGU_GROUP_DIM, (g + 1) * SGU_GROUP_DIM)
            sp = jnp.dot(ws[g], vn[:, cs], preferred_element_type=F32) + bst_ref[:, g:g + 1]
            parts.append((u_ref[rows, cs].astype(F32) * sp).astype(BF16))
        blocks.append(jnp.concatenate(parts, axis=1))
    return jnp.concatenate(blocks, axis=0)


def _merge_kernel(ya_ref, u_ref, vg_ref, ga_ref, gb_ref, x_ref, ws_ref, bst_ref, lng_ref, lnb_ref,
                  wpa_ref, wpb_ref, wo_ref, g_ref, b_ref, o_ref, *, alpha, tm):
    yb = _spatial_gate(u_ref, vg_ref, ws_ref, bst_ref, lng_ref[...], lnb_ref[...], tm)
    pa = jnp.dot(ya_ref[...], wpa_ref[...], preferred_element_type=F32)
    pb = jnp.dot(yb, wpb_ref[...], preferred_element_type=F32)
    m = (jax.nn.sigmoid(ga_ref[...].astype(F32)) * pa
         + jax.nn.sigmoid(gb_ref[...].astype(F32)) * pb)
    mix = jnp.dot(m.astype(BF16), wo_ref[...], preferred_element_type=F32)
    o_ref[...] = _layer_norm(alpha * x_ref[...] + mix, g_ref[...], b_ref[...])


def _merge(ya, u, vg, ga, gb, x, w_s, b_s_t, sgu_g, sgu_b, wpa, wpb, wo, ln_g, ln_b, layer, alpha, tm):
    t_total, d = x.shape
    assert tm % SGU_BLOCK == 0
    kern = functools.partial(_merge_kernel, alpha=alpha, tm=tm)
    row = pl.BlockSpec((tm, d), lambda i: (i, 0))

    return pl.pallas_call(
        kern,
        grid=(t_total // tm,),
        in_specs=[row, row, row, row, row, row,
                  _resident(w_s.shape[1:], layer), _resident(b_s_t.shape[1:], layer),
                  _resident((1, d)), _resident((1, d)),
                  _resident((d, d), layer), _resident((d, d), layer), _resident((d, d), layer),
                  _resident((1, d)), _resident((1, d))],
        out_specs=row,
        out_shape=jax.ShapeDtypeStruct((t_total, d), F32),
        compiler_params=pltpu.CompilerParams(
            dimension_semantics=("arbitrary",), vmem_limit_bytes=VMEM_LIMIT),
        name="merge",
    )(ya, u, vg, ga, gb, x, w_s, b_s_t, sgu_g, sgu_b, wpa, wpb, wo, ln_g, ln_b)


def _ffn_kernel(x_ref, wg_ref, wu_ref, wd_ref, g_ref, b_ref, o_ref, *, alpha):
    xb = x_ref[...].astype(BF16)
    gate = jnp.dot(xb, wg_ref[...], preferred_element_type=F32)
    up = jnp.dot(xb, wu_ref[...], preferred_element_type=F32)
    hid = (_silu(gate) * up).astype(BF16)
    ffn = jnp.dot(hid, wd_ref[...], preferred_element_type=F32)
    o_ref[...] = _layer_norm(alpha * x_ref[...] + ffn, g_ref[...], b_ref[...])


def _ffn(x, wg, wu, wd, ln_g, ln_b, layer, alpha, tm):
    t_total, d = x.shape
    hdim = wg.shape[2]
    kern = functools.partial(_ffn_kernel, alpha=alpha)
    row = pl.BlockSpec((tm, d), lambda i: (i, 0))
    return pl.pallas_call(
        kern,
        grid=(t_total // tm,),
        in_specs=[row,
                  _resident((d, hdim), layer), _resident((d, hdim), layer),
                  _resident((hdim, d), layer), _resident((1, d)), _resident((1, d))],
        out_specs=row,
        out_shape=jax.ShapeDtypeStruct((t_total, d), F32),
        compiler_params=pltpu.CompilerParams(
            dimension_semantics=("arbitrary",), vmem_limit_bytes=VMEM_LIMIT),
        name="ffn",
    )(x, wg, wu, wd, ln_g, ln_b)


def _pick_tile(n, want):
    tile = min(n, want)
    assert n % tile == 0, (n, tile)
    return tile


def kernel(x, w_in, conv_w, a_log, dt_bias, o_norm_w, sgu_ln_g, sgu_ln_b, w_s, b_s, w_pa, w_pb, w_o, ln1_g, ln1_b, w_ffn_gate, w_ffn_up, w_ffn_down, ln2_g, ln2_b):
    batch, seq, d = x.shape
    depth = w_in.shape[0]
    assert d == DN_HEADS * DN_DK == SGU_GROUPS * SGU_GROUP_DIM
    assert seq % SGU_BLOCK == 0
    alpha = (2 * depth) ** 0.25
    t_total = batch * seq

    xf = x.reshape(t_total, d)
    nsm = 2 * DN_HEADS
    w_all = w_in.astype(BF16)
    w_tail = w_all[:, :, 4 * d + nsm:]
    w_small = jnp.pad(w_all[:, :, 4 * d:4 * d + nsm], ((0, 0), (0, 0), (0, LANES - nsm)))
    wpa, wpb, wo = w_pa.astype(BF16), w_pb.astype(BF16), w_o.astype(BF16)
    wg, wu, wd = w_ffn_gate.astype(BF16), w_ffn_up.astype(BF16), w_ffn_down.astype(BF16)
    b_s_t = jnp.swapaxes(b_s, 1, 2)
    lane_pad = ((0, 0), (DN_HEADS, LANES - nsm))
    apar = jnp.stack([jnp.pad(a_log, lane_pad), jnp.pad(dt_bias, lane_pad)], axis=1)
    for l in range(depth):
        qkv, z, u, vg, ga, gb = _in_proj(xf, w_all, w_tail, conv_w, l, seq, _pick_tile(seq, ROWS_IN_PROJ))
        ya = _deltanet(qkv, z, xf, w_small[l], apar[l], o_norm_w[l].reshape(1, DN_DV),
                       batch, seq, d, _pick_tile(seq, ROWS_DELTANET))
        xf = _merge(ya, u, vg, ga, gb, xf, w_s, b_s_t, sgu_ln_g[l].reshape(1, d), sgu_ln_b[l].reshape(1, d),
                    wpa, wpb, wo, ln1_g[l].reshape(1, d), ln1_b[l].reshape(1, d),
                    l, alpha, _pick_tile(t_total, ROWS_MERGE))
        xf = _ffn(xf, wg, wu, wd, ln2_g[l].reshape(1, d), ln2_b[l].reshape(1, d),
                  l, alpha, _pick_tile(t_total, ROWS_FFN))
    return xf.reshape(batch, seq, d)
```

```python
import functools
import math

import jax
import jax.numpy as jnp
from jax import lax
from jax.experimental import pallas as pl
from jax.experimental.pallas import tpu as pltpu

F32 = jnp.float32
BF16 = jnp.bfloat16

CHUNK = 64
DN_HEADS = 8
DN_DK = 128
DN_DV = 128
CONV_K = 4
SGU_BLOCK = 128
SGU_GROUPS = 8
SGU_GROUP_DIM = 128
LN_EPS = 1e-5
RMS_EPS = 1e-6
LOG2E = math.log2(math.e)
CONV_HIST = 16
CONV_BUFS = 4

LANES = 128
MXU_COLS = 256
REST_COLS = 512
VMEM_LIMIT = 56 * 1024 * 1024
ROWS_IN_PROJ = 512
ROWS_DELTANET = 256
ROWS_MERGE_FFN = 256

SEC_Q, SEC_K, SEC_V = range(3)
SEC_Z, SEC_U, SEC_VG, SEC_GA, SEC_GB = range(5)


def _layer_norm(h, g, b):
    mu = jnp.mean(h, -1, keepdims=True)
    d = h - mu
    var = jnp.mean(d * d, -1, keepdims=True)
    return d * lax.rsqrt(var + LN_EPS) * g + b


def _gelu(x):
    return 0.5 * x * (1.0 + lax.erf(x * (1.0 / math.sqrt(2.0))))


def _silu(x):
    return x / (1.0 + jnp.exp2(x * (-LOG2E)))


def _dot(a, b):
    return jnp.dot(a.astype(BF16), b.astype(BF16), preferred_element_type=F32)


def _dot_nt(a, b):
    return lax.dot_general(a.astype(BF16), b.astype(BF16), (((1,), (1,)), ((), ())),
                           preferred_element_type=F32)


def _dot_tn(a, b):
    return lax.dot_general(a.astype(BF16), b.astype(BF16), (((0,), (0,)), ((), ())),
                           preferred_element_type=F32)


def _resident(shape, layer=None):
    if layer is None:
        return pl.BlockSpec(shape, lambda i: (0,) * len(shape), pipeline_mode=pl.Buffered(1))
    return pl.BlockSpec((None,) + tuple(shape), lambda i: (layer,) + (0,) * len(shape),
                        pipeline_mode=pl.Buffered(1))


def _in_proj_kernel(x_ref, xh_ref, whead_ref, wtail_ref, cw_ref, qkv_ref, rest_ref, stage,
                    *, tm, d, tiles_per_seq):
    i = pl.program_id(0)
    keep_hist = (i % tiles_per_seq != 0).astype(F32)
    xb = x_ref[...].astype(BF16)
    xx = jnp.concatenate([xh_ref[...].astype(BF16), xb], axis=0)
    nrest = rest_ref.shape[1]
    assert d % REST_COLS == 0 and nrest // REST_COLS <= 3 * d // MXU_COLS
    for c0 in range(0, 3 * d, MXU_COLS):
        sec = c0 // d
        p = jnp.dot(xx, whead_ref[:, c0:c0 + MXU_COLS], preferred_element_type=F32)
        for s0 in range(0, MXU_COLS, LANES):
            cols = slice(c0 + s0, c0 + s0 + LANES)
            buf = stage.at[(c0 + s0) // LANES % CONV_BUFS]
            buf[0:CONV_HIST, :] = p[0:CONV_HIST, s0:s0 + LANES] * keep_hist
            buf[CONV_HIST:CONV_HIST + tm, :] = p[CONV_HIST:, s0:s0 + LANES]
            w = cw_ref[:, cols]
            acc = buf[CONV_HIST:CONV_HIST + tm, :] * w[CONV_K - 1:CONV_K]
            for j in range(CONV_K - 1):
                off = CONV_HIST - (CONV_K - 1) + j
                acc = acc + buf[off:off + tm, :] * w[j:j + 1]
            y = _silu(acc)
            if sec != SEC_V:
                scale = DN_DK ** -0.5 if sec == SEC_Q else 1.0
                y = y * (lax.rsqrt(jnp.sum(y * y, -1, keepdims=True) + RMS_EPS) * scale)
            qkv_ref[:, cols] = y
        g = c0 // MXU_COLS
        if g < nrest // REST_COLS:
            r0 = g * REST_COLS
            rsec = r0 // d
            if rsec == SEC_Z:
                w = whead_ref[:, 3 * d + r0:3 * d + r0 + REST_COLS]
            else:
                w = wtail_ref[:, r0 - d:r0 - d + REST_COLS]
            r = jnp.dot(xb, w, preferred_element_type=F32)
            if rsec in (SEC_U, SEC_VG):
                r = _gelu(r)
            rest_ref[:, r0:r0 + REST_COLS] = r.astype(rest_ref.dtype)


def _in_proj(x, w_all, w_tail, convw, layer, seq, tm):
    t, d = x.shape
    nrest = d + w_tail.shape[2]
    hist_blocks = tm // CONV_HIST
    kern = functools.partial(_in_proj_kernel, tm=tm, d=d, tiles_per_seq=seq // tm)
    return pl.pallas_call(
        kern,
        grid=(t // tm,),
        in_specs=[pl.BlockSpec((tm, d), lambda i: (i, 0)),
                  pl.BlockSpec((CONV_HIST, d), lambda i: (jnp.maximum(i * hist_blocks - 1, 0), 0)),
                  _resident((d, 4 * d), layer), _resident(w_tail.shape[1:], layer),
                  _resident(convw.shape[1:], layer)],
        out_specs=[pl.BlockSpec((tm, 3 * d), lambda i: (i, 0)),
                   pl.BlockSpec((tm, nrest), lambda i: (i, 0))],
        out_shape=[jax.ShapeDtypeStruct((t, 3 * d), F32),
                   jax.ShapeDtypeStruct((t, nrest), BF16)],
        scratch_shapes=[pltpu.VMEM((CONV_BUFS, CONV_HIST + tm, LANES), F32)],
        compiler_params=pltpu.CompilerParams(
            dimension_semantics=("arbitrary",), vmem_limit_bytes=VMEM_LIMIT),
        name="in_proj",
    )(x, x, w_all, w_tail, convw)


def _deltanet_kernel(qkv_ref, z_ref, x_ref, wsm_ref, apar_ref, onw_ref,
                     y_ref, s_ref, gb_s, bb_s, u_s, wq_s, at_s, kd_s, el_s, *, nb, ts, d):
    t = pl.program_id(0)
    heads = DN_HEADS
    slot = t % 2
    prev = 1 - slot

    @pl.when(t == 0)
    def _():
        s_ref[...] = jnp.zeros_like(s_ref)
        u_s[1] = jnp.zeros_like(u_s[1])
        wq_s[1] = jnp.zeros_like(wq_s[1])
        at_s[1] = jnp.zeros_like(at_s[1])
        kd_s[1] = jnp.zeros_like(kd_s[1])
        el_s[1] = jnp.zeros_like(el_s[1])

    def q_at(b, rows, cols):
        return qkv_ref[b, rows, SEC_Q * d + cols.start:SEC_Q * d + cols.stop]

    def k_at(b, rows, cols):
        return qkv_ref[b, rows, SEC_K * d + cols.start:SEC_K * d + cols.stop]

    def v_at(b, rows, cols):
        return qkv_ref[b, rows, SEC_V * d + cols.start:SEC_V * d + cols.stop]

    hr = range(heads)
    hs = [slice(h * LANES, (h + 1) * LANES) for h in hr]
    pairs = range(heads // 2)
    nchunk = ts // CHUNK
    onw = onw_ref[...]

    tr = lax.broadcasted_iota(jnp.int32, (ts, ts), 0)
    tc = lax.broadcasted_iota(jnp.int32, (ts, ts), 1)
    tril_bd = ((tr // CHUNK == tc // CHUNK) & (tr >= tc)).astype(BF16)
    for b in range(nb):
        sm = jnp.dot(x_ref[b].astype(BF16), wsm_ref[...], preferred_element_type=F32)
        beta_all = jax.nn.sigmoid(sm)
        g_all = -jnp.exp(apar_ref[0:1, :]) * jax.nn.softplus(sm + apar_ref[1:2, :])
        g_hi = g_all.astype(BF16)
        g_mid = (g_all - g_hi.astype(F32)).astype(BF16)
        g_lo = (g_all - g_hi.astype(F32) - g_mid.astype(F32)).astype(BF16)
        gcum_all = (jnp.dot(tril_bd, g_hi, preferred_element_type=F32)
                    + jnp.dot(tril_bd, g_mid, preferred_element_type=F32)
                    + jnp.dot(tril_bd, g_lo, preferred_element_type=F32))
        for h in hr:
            gb_s[b, :, hs[h]] = jnp.broadcast_to(gcum_all[:, heads + h:heads + h + 1], (ts, LANES))
            bb_s[b, :, hs[h]] = jnp.broadcast_to(beta_all[:, h:h + 1], (ts, LANES))

    prow = lax.broadcasted_iota(jnp.int32, (CHUNK, LANES), 0)
    plane = lax.broadcasted_iota(jnp.int32, (CHUNK, LANES), 1)
    pcol = jnp.where(plane < CHUNK, plane, plane - CHUNK)
    left = plane < CHUNK
    tril_p = prow >= pcol
    level_mask = {}
    s = 1
    while s < CHUNK:
        level_mask[s] = ((prow // (2 * s) == pcol // (2 * s))
                         & (prow % (2 * s) >= s) & (pcol % (2 * s) < s))
        s *= 2
    eye_b = prow == pcol
    eye_p = eye_b.astype(F32)
    first_head = lax.broadcasted_iota(jnp.int32, (CHUNK, 2 * LANES), 1) < LANES

    def block_diag(p):
        return jnp.concatenate([jnp.where(left, p, 0.0), jnp.where(left, 0.0, p)], axis=0)

    def prep_chunks():
        units = [(b, c, p) for b in range(nb) for c in range(nchunk) for p in pairs]
        rows = [pl.ds(c * CHUNK, CHUNK) for c in range(nchunk)]
        last = [pl.ds((c + 1) * CHUNK - 1, 1) for c in range(nchunk)]

        a, dec, kb = {}, {}, {}
        for u in units:
            b, c, p = u
            h1, h2 = 2 * p, 2 * p + 1
            cols = slice(h1 * LANES, (h2 + 1) * LANES)
            gi = jnp.where(left, gb_s[b, rows[c], hs[h1]], gb_s[b, rows[c], hs[h2]])
            gj = jnp.sum(jnp.where(eye_b, gi, 0.0), axis=0, keepdims=True)
            dec[u] = jnp.where(tril_p, jnp.exp(jnp.minimum(gi - gj, 0.0)), 0.0)
            k2 = k_at(b, rows[c], cols)
            kb[u] = k2 * bb_s[b, rows[c], cols]
            lhs = jnp.concatenate([kb[u], q_at(b, rows[c], cols)], axis=0)
            rhs = jnp.concatenate([jnp.where(first_head, k2, 0.0),
                                   jnp.where(first_head, 0.0, k2)], axis=0)
            a[u] = _dot_nt(lhs, rhs)
        pneg, tinv = {}, {}
        for u in units:
            b, c, p = u
            attn = a[u][CHUNK:] * dec[u]
            at_s[slot, b, c, 2 * p] = attn[:, :CHUNK].astype(BF16)
            at_s[slot, b, c, 2 * p + 1] = attn[:, CHUNK:].astype(BF16)
            pneg[u] = -(a[u][:CHUNK] * dec[u])
            tinv[u] = eye_p + jnp.where(level_mask[1], pneg[u], 0.0)
        s = 2
        while s < CHUNK:
            y = {u: _dot(jnp.where(level_mask[s], pneg[u], 0.0), block_diag(tinv[u])) for u in units}
            for u in units:
                tinv[u] = tinv[u] + _dot(tinv[u], block_diag(y[u]))
            s *= 2
        for u in units:
            b, c, p = u
            for j, h in enumerate((2 * p, 2 * p + 1)):
                th = tinv[u][:, j * CHUNK:(j + 1) * CHUNK]
                gi = gb_s[b, rows[c], hs[h]]
                glast = gb_s[b, last[c], hs[h]]
                eg = jnp.exp(gi)
                kbh = kb[u][:, j * LANES:(j + 1) * LANES]
                uw = _dot(th, jnp.concatenate(
                    [v_at(b, rows[c], hs[h]) * bb_s[b, rows[c], hs[h]], kbh * eg], axis=1))
                u_s[slot, b, c, h] = uw[:, :DN_DV]
                wq_s[slot, b, c, h] = jnp.concatenate(
                    [uw[:, DN_DV:], q_at(b, rows[c], hs[h]) * eg], axis=0).astype(BF16)
                kd_s[slot, b, c, h] = (k_at(b, rows[c], hs[h]) * jnp.exp(glast - gi)).astype(BF16)
                el_s[slot, b, c, h:h + 1, :] = jnp.exp(glast)

    def scan_chunk(c):
        rows = pl.ds(c * CHUNK, CHUNK)
        chains = [(b, h) for b in range(nb) for h in hr]
        s_old = {bh: s_ref[bh[0], bh[1]] for bh in chains}
        sw = {(b, h): _dot(wq_s[prev, b, c, h], s_old[b, h]) for (b, h) in chains}
        v_new = {(b, h): u_s[prev, b, c, h] - sw[b, h][:CHUNK] for (b, h) in chains}
        o = {(b, h): sw[b, h][CHUNK:] + _dot(at_s[prev, b, c, h], v_new[b, h]) for (b, h) in chains}
        for (b, h) in chains:
            s_ref[b, h] = (s_old[b, h] * el_s[prev, b, c, h:h + 1, :]
                           + _dot_tn(kd_s[prev, b, c, h], v_new[b, h]))
        for (b, h) in chains:
            z = z_ref[b, rows, hs[h]].astype(F32)
            ob = o[b, h]
            y = ob * lax.rsqrt(jnp.mean(ob * ob, -1, keepdims=True) + RMS_EPS) * onw * _silu(z)
            y_ref[b, rows, hs[h]] = y.astype(y_ref.dtype)

    for c in range(nchunk):
        scan_chunk(c)
    prep_chunks()


def _deltanet(qkv, rest, x, w_small, apar, onw, batch, seq, d, ts):
    nt = seq // ts
    nc = ts // CHUNK
    kern = functools.partial(_deltanet_kernel, nb=batch, ts=ts, d=d)

    def cur(t):
        return jnp.minimum(t, nt - 1)

    def lag(t):
        return jnp.maximum(t - 1, 0)

    y = pl.pallas_call(
        kern,
        grid=(nt + 1,),
        in_specs=[
            pl.BlockSpec((batch, ts, 3 * d), lambda t: (0, cur(t), 0)),
            pl.BlockSpec((batch, ts, d), lambda t: (0, lag(t), SEC_Z)),
            pl.BlockSpec((batch, ts, d), lambda t: (0, cur(t), 0)),
            pl.BlockSpec((d, LANES), lambda t: (0, 0)),
            pl.BlockSpec((2, LANES), lambda t: (0, 0)),
            pl.BlockSpec((1, LANES), lambda t: (0, 0)),
        ],
        out_specs=pl.BlockSpec((batch, ts, d), lambda t: (0, lag(t), 0)),
        out_shape=jax.ShapeDtypeStruct((batch, seq, d), BF16),
        scratch_shapes=[
            pltpu.VMEM((batch, DN_HEADS, DN_DK, DN_DV), F32),
            pltpu.VMEM((batch, ts, d), F32),
            pltpu.VMEM((batch, ts, d), F32),
            pltpu.VMEM((2, batch, nc, DN_HEADS, CHUNK, DN_DV), F32),
            pltpu.VMEM((2, batch, nc, DN_HEADS, 2 * CHUNK, DN_DK), BF16),
            pltpu.VMEM((2, batch, nc, DN_HEADS, CHUNK, CHUNK), BF16),
            pltpu.VMEM((2, batch, nc, DN_HEADS, CHUNK, DN_DK), BF16),
            pltpu.VMEM((2, batch, nc, DN_HEADS, LANES), F32),
        ],
        compiler_params=pltpu.CompilerParams(
            dimension_semantics=("arbitrary",), vmem_limit_bytes=VMEM_LIMIT),
        name="deltanet",
    )(qkv.reshape(batch, seq, 3 * d), rest.reshape(batch, seq, -1), x.reshape(batch, seq, d),
      w_small, apar, onw)
    return y.reshape(batch * seq, d)


def _spatial_gate(u_ref, vg_ref, ws_ref, bst_ref, lng, lnb, tm):
    ri = lax.broadcasted_iota(jnp.int32, (SGU_BLOCK, SGU_BLOCK), 0) // CHUNK
    ci = lax.broadcasted_iota(jnp.int32, (SGU_BLOCK, SGU_BLOCK), 1) // CHUNK
    causal = ri >= ci
    ws = [jnp.where(causal, ws_ref[g], 0.0).astype(BF16) for g in range(SGU_GROUPS)]
    blocks = []
    for n in range(tm // SGU_BLOCK):
        rows = slice(n * SGU_BLOCK, (n + 1) * SGU_BLOCK)
        vn = _layer_norm(vg_ref[rows, :].astype(F32), lng, lnb).astype(BF16)
        parts = []
        for g in range(SGU_GROUPS):
            cs = slice(g * SGU_GROUP_DIM, (g + 1) * SGU_GROUP_DIM)
            sp = jnp.dot(ws[g], vn[:, cs], preferred_element_type=F32) + bst_ref[:, g:g + 1]
            parts.append((u_ref[rows, cs].astype(F32) * sp).astype(BF16))
        blocks.append(jnp.concatenate(parts, axis=1))
    return jnp.concatenate(blocks, axis=0)


def _merge_ffn_kernel(ya_ref, u_ref, vg_ref, ga_ref, gb_ref, x_ref, ws_ref, bst_ref, lng_ref, lnb_ref,
                      wpa_ref, wpb_ref, wo_ref, g1_ref, b1_ref, wg_ref, wu_ref, wd_ref, g2_ref, b2_ref,
                      o_ref, *, alpha, tm):
    yb = _spatial_gate(u_ref, vg_ref, ws_ref, bst_ref, lng_ref[...], lnb_ref[...], tm)
    pa = jnp.dot(ya_ref[...], wpa_ref[...], preferred_element_type=F32)
    pb = jnp.dot(yb, wpb_ref[...], preferred_element_type=F32)
    m = (jax.nn.sigmoid(ga_ref[...].astype(F32)) * pa
         + jax.nn.sigmoid(gb_ref[...].astype(F32)) * pb)
    mix = jnp.dot(m.astype(BF16), wo_ref[...], preferred_element_type=F32)
    x1 = _layer_norm(alpha * x_ref[...] + mix, g1_ref[...], b1_ref[...])
    xb = x1.astype(BF16)
    gate = jnp.dot(xb, wg_ref[...], preferred_element_type=F32)
    up = jnp.dot(xb, wu_ref[...], preferred_element_type=F32)
    hid = (_silu(gate) * up).astype(BF16)
    ffn = jnp.dot(hid, wd_ref[...], preferred_element_type=F32)
    o_ref[...] = _layer_norm(alpha * x1 + ffn, g2_ref[...], b2_ref[...])


def _merge_ffn(ya, rest, x, w_s, b_s_t, sgu_g, sgu_b, wpa, wpb, wo, ln1_g, ln1_b,
               wg, wu, wd, ln2_g, ln2_b, layer, alpha, tm):
    t_total, d = x.shape
    hdim = wg.shape[2]
    assert tm % SGU_BLOCK == 0
    kern = functools.partial(_merge_ffn_kernel, alpha=alpha, tm=tm)
    row = pl.BlockSpec((tm, d), lambda i: (i, 0))

    def sec(s):
        return pl.BlockSpec((tm, d), lambda i: (i, s))

    return pl.pallas_call(
        kern,
        grid=(t_total // tm,),
        in_specs=[row, sec(SEC_U), sec(SEC_VG), sec(SEC_GA), sec(SEC_GB), row,
                  _resident(w_s.shape[1:], layer), _resident(b_s_t.shape[1:], layer),
                  _resident((1, d)), _resident((1, d)),
                  _resident((d, d), layer), _resident((d, d), layer), _resident((d, d), layer),
                  _resident((1, d)), _resident((1, d)),
                  _resident((d, hdim), layer), _resident((d, hdim), layer), _resident((hdim, d), layer),
                  _resident((1, d)), _resident((1, d))],
        out_specs=row,
        out_shape=jax.ShapeDtypeStruct((t_total, d), F32),
        compiler_params=pltpu.CompilerParams(
            dimension_semantics=("arbitrary",), vmem_limit_bytes=VMEM_LIMIT),
        name="merge_ffn",
    )(ya, rest, rest, rest, rest, x, w_s, b_s_t, sgu_g, sgu_b, wpa, wpb, wo, ln1_g, ln1_b,
      wg, wu, wd, ln2_g, ln2_b)


def _pick_tile(n, want):
    tile = min(n, want)
    assert n % tile == 0, (n, tile)
    return tile


def kernel(x, w_in, conv_w, a_log, dt_bias, o_norm_w, sgu_ln_g, sgu_ln_b, w_s, b_s, w_pa, w_pb, w_o, ln1_g, ln1_b, w_ffn_gate, w_ffn_up, w_ffn_down, ln2_g, ln2_b):
    batch, seq, d = x.shape
    depth = w_in.shape[0]
    assert d == DN_HEADS * DN_DK == SGU_GROUPS * SGU_GROUP_DIM
    assert seq % SGU_BLOCK == 0
    alpha = (2 * depth) ** 0.25
    t_total = batch * seq

    xf = x.reshape(t_total, d)
    nsm = 2 * DN_HEADS
    w_all = w_in.astype(BF16)
    w_tail = w_all[:, :, 4 * d + nsm:]
    w_small = jnp.pad(w_all[:, :, 4 * d:4 * d + nsm], ((0, 0), (0, 0), (0, LANES - nsm)))
    wpa, wpb, wo = w_pa.astype(BF16), w_pb.astype(BF16), w_o.astype(BF16)
    wg, wu, wd = w_ffn_gate.astype(BF16), w_ffn_up.astype(BF16), w_ffn_down.astype(BF16)
    b_s_t = jnp.swapaxes(b_s, 1, 2)
    lane_pad = ((0, 0), (DN_HEADS, LANES - nsm))
    apar = jnp.stack([jnp.pad(a_log, lane_pad), jnp.pad(dt_bias, lane_pad)], axis=1)
    for l in range(depth):
        qkv, rest = _in_proj(xf, w_all, w_tail, conv_w, l, seq, _pick_tile(seq, ROWS_IN_PROJ))
        ya = _deltanet(qkv, rest, xf, w_small[l], apar[l], o_norm_w[l].reshape(1, DN_DV),
                       batch, seq, d, _pick_tile(seq, ROWS_DELTANET))
        xf = _merge_ffn(ya, rest, xf, w_s, b_s_t, sgu_ln_g[l].reshape(1, d), sgu_ln_b[l].reshape(1, d),
                        wpa, wpb, wo, ln1_g[l].reshape(1, d), ln1_b[l].reshape(1, d),
                        wg, wu, wd, ln2_g[l].reshape(1, d), ln2_b[l].reshape(1, d),
                        l, alpha, _pick_tile(t_total, ROWS_MERGE_FFN))
    return xf.reshape(batch, seq, d)
```
